```python
import math
import jax, jax.numpy as jnp
from jax import lax
import numpy as np

D_MODEL = 1024
BATCH = 4
SEQ = 4096
DEPTH = 4

GRID_W = 64
CTX_LEN = 256
D_HY = D_MODEL
N_BANDS = 16
FILTER_EMB = 1 + 2 * N_BANDS
FILTER_ORDER = 64
SHORT_W = 3
GLA_HEADS = 4
GLA_DK = D_MODEL // 8
GLA_DV = D_MODEL // 4
GLA_DK_TOT = GLA_HEADS * GLA_DK
GLA_DV_TOT = GLA_HEADS * GLA_DV
GATE_RANK = 16
GATE_TAU = 16.0
GLA_CHUNK = 64
D_FF = 4 * D_MODEL
EPS = 1e-6
IN_SIZES = (3 * D_HY, GLA_DK_TOT, GLA_DK_TOT, GLA_DV_TOT, GLA_DV_TOT, GATE_RANK, GATE_RANK, D_MODEL, D_MODEL)
IN_WIDTH = sum(IN_SIZES)
IN_SPLITS = tuple(np.cumsum(IN_SIZES)[:-1].tolist())

kernel_name = "hyena_gla_parallel_gated_dit_trunk"


def rmsnorm(x, g):
    xf = x.astype(jnp.float32)
    y = xf * lax.rsqrt(jnp.mean(xf * xf, axis=-1, keepdims=True) + EPS)
    return (y * g.astype(jnp.float32)).astype(x.dtype)


def conv3(x, w, b):
    L = x.shape[-2]
    xp = jnp.pad(x, [(0, 0)] * (x.ndim - 2) + [(1, 1), (0, 0)])
    return xp[..., :L, :] * w[0] + xp[..., 1:L + 1, :] * w[1] + xp[..., 2:, :] * w[2] + b


def hyena_filter(L, w1, b1, w2, b2, w3, freq, decay):
    f32 = jnp.float32
    t = jnp.linspace(0.0, 1.0, L, dtype=f32)[:, None]
    ang = (2.0 * math.pi / L) * jnp.arange(L, dtype=f32)[:, None] * \
        jnp.linspace(1e-4, N_BANDS - 1, N_BANDS, dtype=f32)[None, :]
    z = jnp.concatenate([t, jnp.cos(ang), -jnp.sin(ang)], axis=-1)
    fr = freq.astype(f32)
    a = jnp.sin(fr * (z @ w1.astype(f32) + b1.astype(f32)))
    a = jnp.sin(fr * (a @ w2.astype(f32) + b2.astype(f32)))
    hk = a @ w3.astype(f32)
    window = jnp.exp(-t * jnp.abs(decay.astype(f32))[None, :])
    h_f = hk[:, :D_HY] * window
    h_b = hk[:, D_HY:] * window
    return jnp.concatenate([h_f, jnp.zeros((1, D_HY), f32), h_b[:0:-1]], axis=0)


def long_conv(u, hcirc):
    L = u.shape[1]
    U = jnp.fft.rfft(u.astype(jnp.float32), n=2 * L, axis=1)
    H = jnp.fft.rfft(hcirc, axis=0)
    return jnp.fft.irfft(U * H[None], n=2 * L, axis=1)[:, :L].astype(u.dtype)


def hyena_branch(hy_p, hcirc, conv_w, conv_b, skip, rows):
    B, L, C3 = hy_p.shape
    if rows is None:
        u = conv3(hy_p, conv_w, conv_b)
    else:
        u = conv3(hy_p.reshape(B, rows, GRID_W, C3), conv_w, conv_b).reshape(B, L, C3)
    x0, x1, v = jnp.split(u, 3, axis=-1)
    z = x1 * v
    return x0 * (long_conv(z, hcirc) + skip * z)


def gla_inputs(q, k, v, lr_f, lr_b, gate_w, gate_b):
    B, L, _ = q.shape

    def heads(a):
        return a.reshape(B, L, GLA_HEADS, -1).transpose(0, 2, 1, 3).astype(jnp.float32)

    gf = jax.nn.log_sigmoid((lr_f @ gate_w[0] + gate_b[0]).astype(jnp.float32)) / GATE_TAU
    gb = jax.nn.log_sigmoid((lr_b @ gate_w[1] + gate_b[1]).astype(jnp.float32)) / GATE_TAU
    return heads(q) * (GLA_DK ** -0.5), heads(k), heads(v), heads(gf), heads(gb)


def gla_chunk_scan(q, k, v, g, s0, with_out):
    B, H, L, DK = q.shape
    DV = v.shape[-1]
    n = L // GLA_CHUNK

    def to_chunks(a):
        return jnp.moveaxis(a.reshape(B, H, n, GLA_CHUNK, a.shape[-1]), 2, 0)

    lower = jnp.tril(jnp.ones((GLA_CHUNK, GLA_CHUNK), dtype=bool))

    def step(S, blk):
        qb, kb, vb, gb = blk
        b = jnp.cumsum(gb, axis=2)
        b_end = b[:, :, -1:, :]
        S_next = jnp.exp(b_end)[:, :, 0, :, None] * S + \
            jnp.einsum('bhcd,bhce->bhde', kb * jnp.exp(b_end - b), vb)
        if not with_out:
            return S_next, None
        q_t = qb * jnp.exp(b)
        k_t = kb * jnp.exp(-b)
        att = jnp.where(lower, jnp.einsum('bhtd,bhsd->bhts', q_t, k_t), 0.0)
        o = jnp.einsum('bhts,bhse->bhte', att, vb) + jnp.einsum('bhtd,bhde->bhte', q_t, S)
        return S_next, o

    S_fin, o = lax.scan(step, s0, (to_chunks(q), to_chunks(k), to_chunks(v), to_chunks(g)))
    if with_out:
        o = jnp.moveaxis(o, 0, 2).reshape(B, H, L, DV)
    return S_fin, o


def gla_bidir(q, k, v, gf, gb, s0_f, s0_b, with_out):
    flip = lambda a: jnp.flip(a, axis=2)
    s_f, o_f = gla_chunk_scan(q, k, v, gf, s0_f, with_out)
    s_b, o_b = gla_chunk_scan(flip(q), flip(k), flip(v), flip(gb), s0_b, with_out)
    o = o_f + flip(o_b) if with_out else None
    return o, s_f, s_b


def gla_output(o, r, g):
    B, H, L, DV = o.shape
    on = o * lax.rsqrt(jnp.mean(o * o, axis=-1, keepdims=True) + EPS) * g.astype(jnp.float32)
    on = on.transpose(0, 2, 1, 3).reshape(B, L, H * DV).astype(r.dtype)
    return on * jax.nn.silu(r)


def mixer_output(hy_p, r, gt_hy, gt_gla, o_gla, hcirc, rows, conv_w, conv_b, skip, norm_g, w_bhy, w_bgla, w_o):
    y_hy = hyena_branch(hy_p, hcirc, conv_w, conv_b, skip, rows)
    y_gla = gla_output(o_gla, r, norm_g)
    merged = jax.nn.sigmoid(gt_hy) * (y_hy @ w_bhy) + jax.nn.sigmoid(gt_gla) * (y_gla @ w_bgla)
    return merged @ w_o


def sq_relu_mlp(h, w1, w2):
    return jnp.square(jax.nn.relu(h @ w1)) @ w2


def setup_inputs(seed: int = 0) -> dict:
    key = jax.random.key(seed)
    ks = jax.random.split(key, 32)
    f32 = jnp.float32

    def nrm(k, shape, s):
        return jax.random.normal(k, shape, f32) * s

    decay_base = jnp.linspace(-math.log(1e-2) / 1.5, -math.log(1e-2) / 0.3, D_HY, dtype=f32)
    return {
        "x": nrm(ks[0], (BATCH, SEQ, D_MODEL), 1.0),
        "c": nrm(ks[1], (BATCH, D_MODEL), 1.0),
        "ctx": nrm(ks[2], (BATCH, CTX_LEN, D_MODEL), 1.0),
        "c_ctx": nrm(ks[3], (D_MODEL,), 1.0),
        "ada_w": nrm(ks[4], (DEPTH, D_MODEL, 6 * D_MODEL), 0.5 * D_MODEL ** -0.5),
        "ada_b": nrm(ks[5], (DEPTH, 6 * D_MODEL), 0.02),
        "norm1_g": 1.0 + nrm(ks[6], (DEPTH, D_MODEL), 0.02),
        "norm2_g": 1.0 + nrm(ks[7], (DEPTH, D_MODEL), 0.02),
        "w_in": nrm(ks[8], (DEPTH, D_MODEL, IN_WIDTH), D_MODEL ** -0.5),
        "hy_conv_w": nrm(ks[9], (DEPTH, SHORT_W, 3 * D_HY), SHORT_W ** -0.5),
        "hy_conv_b": nrm(ks[10], (DEPTH, 3 * D_HY), 0.02),
        "hy_filt_w1": nrm(ks[11], (DEPTH, FILTER_EMB, FILTER_ORDER), FILTER_EMB ** -0.5),
        "hy_filt_b1": nrm(ks[12], (DEPTH, FILTER_ORDER), 0.1),
        "hy_filt_w2": nrm(ks[13], (DEPTH, FILTER_ORDER, FILTER_ORDER), FILTER_ORDER ** -0.5),
        "hy_filt_b2": nrm(ks[14], (DEPTH, FILTER_ORDER), 0.1),
        "hy_filt_w3": nrm(ks[15], (DEPTH, FILTER_ORDER, 2 * D_HY), 0.05 * FILTER_ORDER ** -0.5),
        "hy_filt_freq": 1.0 + nrm(ks[16], (DEPTH, FILTER_ORDER), 0.05),
        "hy_decay": decay_base[None, :] * (1.0 + nrm(ks[17], (DEPTH, D_HY), 0.05)),
        "hy_skip": nrm(ks[18], (DEPTH, D_HY), 1.0),
        "gla_gate_w": nrm(ks[19], (DEPTH, 2, GATE_RANK, GLA_DK_TOT), GATE_RANK ** -0.5),
        "gla_gate_b": nrm(ks[20], (DEPTH, 2, GLA_DK_TOT), 0.1),
        "gla_norm_g": 1.0 + nrm(ks[21], (DEPTH, GLA_DV), 0.02),
        "w_branch_hy": nrm(ks[22], (DEPTH, D_HY, D_MODEL), D_HY ** -0.5),
        "w_branch_gla": nrm(ks[23], (DEPTH, GLA_DV_TOT, D_MODEL), GLA_DV_TOT ** -0.5),
        "w_out": nrm(ks[24], (DEPTH, D_MODEL, D_MODEL), D_MODEL ** -0.5),
        "mlp_w1": nrm(ks[25], (DEPTH, D_MODEL, D_FF), D_MODEL ** -0.5),
        "mlp_w2": nrm(ks[26], (DEPTH, D_FF, D_MODEL), D_FF ** -0.5),
        "final_g": 1.0 + nrm(ks[27], (D_MODEL,), 0.02),
    }


def reference(x, c, ctx, c_ctx, ada_w, ada_b, norm1_g, norm2_g, w_in, hy_conv_w, hy_conv_b,
              hy_filt_w1, hy_filt_b1, hy_filt_w2, hy_filt_b2, hy_filt_w3, hy_filt_freq, hy_decay, hy_skip,
              gla_gate_w, gla_gate_b, gla_norm_g, w_branch_hy, w_branch_gla, w_out, mlp_w1, mlp_w2, final_g):
    B, L, _ = x.shape
    L_ctx = ctx.shape[1]
    rows = L // GRID_W
    silu_c = jax.nn.silu(c)
    silu_cc = jax.nn.silu(c_ctx)
    for l in range(DEPTH):
        last = l == DEPTH - 1
        mod = (silu_c @ ada_w[l] + ada_b[l])[:, None, :]
        mod_c = (silu_cc @ ada_w[l] + ada_b[l])[None, None, :]
        sh1, sc1, g1, sh2, sc2, g2 = jnp.split(mod, 6, axis=-1)
        csh1, csc1, cg1, csh2, csc2, cg2 = jnp.split(mod_c, 6, axis=-1)

        h = rmsnorm(x, norm1_g[l]) * (1 + sc1) + sh1
        hc = rmsnorm(ctx, norm1_g[l]) * (1 + csc1) + csh1
        hy_p, q, k, v, r, lr_f, lr_b, gt_hy, gt_gla = jnp.split(h @ w_in[l], IN_SPLITS, axis=-1)
        hy_pc, qc, kc, vc, rc, lr_fc, lr_bc, gt_hyc, gt_glac = jnp.split(hc @ w_in[l], IN_SPLITS, axis=-1)

        zeros_state = jnp.zeros((B, GLA_HEADS, GLA_DK, GLA_DV), jnp.float32)
        q_c, k_c, v_c, gf_c, gb_c = gla_inputs(qc, kc, vc, lr_fc, lr_bc, gla_gate_w[l], gla_gate_b[l])
        o_c, s_f_ctx, s_b_ctx = gla_bidir(q_c, k_c, v_c, gf_c, gb_c, zeros_state, zeros_state, not last)
        q_l, k_l, v_l, gf_l, gb_l = gla_inputs(q, k, v, lr_f, lr_b, gla_gate_w[l], gla_gate_b[l])
        o_l, _, _ = gla_bidir(q_l, k_l, v_l, gf_l, gb_l, s_f_ctx, s_b_ctx, True)

        filt = (hy_filt_w1[l], hy_filt_b1[l], hy_filt_w2[l], hy_filt_b2[l], hy_filt_w3[l], hy_filt_freq[l], hy_decay[l])
        shared = (hy_conv_w[l], hy_conv_b[l], hy_skip[l], gla_norm_g[l], w_branch_hy[l], w_branch_gla[l], w_out[l])
        out = mixer_output(hy_p, r, gt_hy, gt_gla, o_l, hyena_filter(L, *filt), rows, *shared)
        x = x + g1 * out
        if not last:
            out_c = mixer_output(hy_pc, rc, gt_hyc, gt_glac, o_c, hyena_filter(L_ctx, *filt), None, *shared)
            ctx = ctx + cg1 * out_c

        h2 = rmsnorm(x, norm2_g[l]) * (1 + sc2) + sh2
        x = x + g2 * sq_relu_mlp(h2, mlp_w1[l], mlp_w2[l])
        if not last:
            h2c = rmsnorm(ctx, norm2_g[l]) * (1 + csc2) + csh2
            ctx = ctx + cg2 * sq_relu_mlp(h2c, mlp_w1[l], mlp_w2[l])
    return rmsnorm(x, final_g)
```

```python
import functools
import math

import numpy as np
import jax
import jax.numpy as jnp
from jax import lax
from jax.experimental import pallas as pl
from jax.experimental.pallas import tpu as pltpu

F32 = jnp.float32
BF16 = jnp.bfloat16

GRID_W = 64
N_BANDS = 16
GLA_HEADS = 4
GLA_CHUNK = 64
GATE_TAU = 16.0
EPS = 1e-6
N_MOD = 6

V7X_VMEM_BYTES = 64 * 1024 * 1024
VMEM_LIMIT_BYTES = V7X_VMEM_BYTES * 3 // 4
LANES = 128
COND_ROWS = 8


def _cparams(*sem):
    return pltpu.CompilerParams(dimension_semantics=sem, vmem_limit_bytes=VMEM_LIMIT_BYTES)


def _tile(n, pref):
    t = min(n, pref)
    while n % t:
        t //= 2
    return t


def _sigmoid(x):
    return 1.0 / (1.0 + jnp.exp(-x))


def _silu(x):
    return x * _sigmoid(x)


def _log_sigmoid(x):
    return jnp.minimum(x, 0.0) - jnp.log(1.0 + jnp.exp(-jnp.abs(x)))


def _modulated_norm(x, g, mod, shift_idx, scale_idx, d):
    y = x * lax.rsqrt(jnp.mean(x * x, axis=-1, keepdims=True) + EPS) * g
    shift = mod[:, shift_idx * d:(shift_idx + 1) * d]
    scale = mod[:, scale_idx * d:(scale_idx + 1) * d]
    return y * (1.0 + scale) + shift


def _ada_body(c_ref, w_ref, b_ref, o_ref):
    s = _silu(c_ref[...]).astype(BF16)
    o_ref[...] = jnp.dot(s, w_ref[...].astype(BF16), preferred_element_type=F32) + b_ref[...]


def _ada_mod(cond, ada_w, ada_b):
    depth, d, n = ada_w.shape
    tn = _tile(n, 1536)
    return pl.pallas_call(
        _ada_body,
        grid=(depth, n // tn),
        in_specs=[
            pl.BlockSpec((COND_ROWS, d), lambda l, j: (0, 0)),
            pl.BlockSpec((None, d, tn), lambda l, j: (l, 0, j)),
            pl.BlockSpec((None, 1, tn), lambda l, j: (l, 0, j)),
        ],
        out_specs=pl.BlockSpec((None, COND_ROWS, tn), lambda l, j: (l, 0, j)),
        out_shape=jax.ShapeDtypeStruct((depth, COND_ROWS, n), F32),
        compiler_params=_cparams("parallel", "parallel"),
        name="ada_mod",
    )(cond, ada_w, ada_b.reshape(depth, 1, n))


def _in_proj_body(x_ref, mod_ref, g_ref, w_ref, wlr_ref, o_ref, olr_ref, h_scr, *, d):
    @pl.when(pl.program_id(1) == 0)
    def _():
        h = _modulated_norm(x_ref[...], g_ref[...], mod_ref[...], 0, 1, d).astype(BF16)
        h_scr[...] = h
        olr_ref[...] = jnp.dot(h, wlr_ref[...], preferred_element_type=F32)

    o_ref[...] = jnp.dot(h_scr[...], w_ref[...], preferred_element_type=F32)


def _in_proj(x2, mod_l, norm_g, w_main, w_lr, rows_per_cond, cond_base):
    r, d = x2.shape
    n = w_main.shape[1]
    tm = _tile(min(r, rows_per_cond), 1024)
    tn = _tile(n, 1024)
    blocks_per_cond = rows_per_cond // tm
    return pl.pallas_call(
        functools.partial(_in_proj_body, d=d),
        grid=(r // tm, n // tn),
        in_specs=[
            pl.BlockSpec((tm, d), lambda i, j: (i, 0)),
            pl.BlockSpec((None, 1, N_MOD * d), lambda i, j: (cond_base + i // blocks_per_cond, 0, 0)),
            pl.BlockSpec((1, d), lambda i, j: (0, 0)),
            pl.BlockSpec((d, tn), lambda i, j: (0, j)),
            pl.BlockSpec((d, LANES), lambda i, j: (0, 0)),
        ],
        out_specs=[
            pl.BlockSpec((tm, tn), lambda i, j: (i, j)),
            pl.BlockSpec((tm, LANES), lambda i, j: (i, 0)),
        ],
        out_shape=[jax.ShapeDtypeStruct((r, n), F32), jax.ShapeDtypeStruct((r, LANES), F32)],
        scratch_shapes=[pltpu.VMEM((tm, d), BF16)],
        compiler_params=_cparams("parallel", "arbitrary"),
        name="in_proj",
    )(x2, mod_l, norm_g, w_main, w_lr)


def _filter_body(feat_ref, w1_ref, b1_ref, w2_ref, b2_ref, w3_ref, fr_ref, dec_ref, ed_ref, hl_ref, *, c, tl):
    i = pl.program_id(0)
    hp = lax.Precision.HIGHEST
    feat = feat_ref[...]
    fr = fr_ref[...]
    a = jnp.sin(fr * (jnp.dot(feat, w1_ref[...], precision=hp, preferred_element_type=F32) + b1_ref[...]))
    a = jnp.sin(fr * (jnp.dot(a, w2_ref[...], precision=hp, preferred_element_type=F32) + b2_ref[...]))
    hk = jnp.dot(a, w3_ref[...], precision=hp, preferred_element_type=F32)
    t = feat[:, 0:1]
    window = jnp.exp(-t * jnp.abs(dec_ref[...]))
    h_f = hk[:, :c] * window
    h_b = hk[:, c:] * window
    row = i * tl + lax.broadcasted_iota(jnp.int32, (tl, 1), 0)
    h_b = jnp.where(row == 0, 0.0, h_b)
    e = h_f + h_b
    ed_ref[:, :c] = e.astype(BF16)
    ed_ref[:, c:] = (h_f - h_b).astype(BF16)
    sign = jnp.where(row % 2 == 0, 1.0, -1.0)
    part = jnp.sum(e * sign, axis=0, keepdims=True)

    @pl.when(i == 0)
    def _():
        hl_ref[...] = jnp.zeros_like(hl_ref)

    hl_ref[...] += part


def _pad_to(a, shape):
    return jnp.pad(a, [(0, s - n) for n, s in zip(a.shape, shape)])


def _hyena_filter_parts(feat, w1, b1, w2, b2, w3, freq, decay):
    l = feat.shape[0]
    c = decay.shape[-1]
    fe = fo = LANES
    assert feat.shape[1] <= fe and w1.shape[1] <= fo
    feat = _pad_to(feat, (l, fe))
    w1, w2, w3 = _pad_to(w1, (fe, fo)), _pad_to(w2, (fo, fo)), _pad_to(w3, (fo, 2 * c))
    b1, b2, freq = _pad_to(b1, (fo,)), _pad_to(b2, (fo,)), _pad_to(freq, (fo,))
    tl = _tile(l, 512)
    full = lambda shape: pl.BlockSpec(shape, lambda i: (0, 0))
    return pl.pallas_call(
        functools.partial(_filter_body, c=c, tl=tl),
        grid=(l // tl,),
        in_specs=[
            pl.BlockSpec((tl, fe), lambda i: (i, 0)),
            full((fe, fo)), full((1, fo)), full((fo, fo)), full((1, fo)), full((fo, 2 * c)),
            full((1, fo)), full((1, c)),
        ],
        out_specs=[pl.BlockSpec((tl, 2 * c), lambda i: (i, 0)), pl.BlockSpec((1, c), lambda i: (0, 0))],
        out_shape=[jax.ShapeDtypeStruct((l, 2 * c), BF16), jax.ShapeDtypeStruct((1, c), F32)],
        compiler_params=_cparams("arbitrary"),
        name="hyena_filter",
    )(feat, w1, b1.reshape(1, fo), w2, b2.reshape(1, fo), w3, freq.reshape(1, fo), decay.reshape(1, c))


def _filter_features(l):
    t = jnp.linspace(0.0, 1.0, l, dtype=F32)[:, None]
    ang = (2.0 * math.pi / l) * jnp.arange(l, dtype=F32)[:, None] * \
        jnp.linspace(1e-4, N_BANDS - 1, N_BANDS, dtype=F32)[None, :]
    return jnp.concatenate([t, jnp.cos(ang), -jnp.sin(ang)], axis=-1)


def _dft_index(l, th):
    rho = jnp.arange(2 * l, dtype=jnp.int32)
    blk = rho // (2 * th)
    part = (rho // th) % 2
    k = blk * th + rho % th
    return k, part


def _dft_forward_matrix(l, th):
    n = 2 * l
    k, part = _dft_index(l, th)
    t = jnp.arange(l, dtype=jnp.int32)[None, :]
    k, part = k[:, None], part[:, None]
    nyq = (k == 0) & (part == 1)
    kk = jnp.where(nyq, l, k)
    ang = ((kk * t) % n).astype(F32) * (2.0 * math.pi / n)
    m = jnp.where((part == 0) | nyq, jnp.cos(ang), -jnp.sin(ang))
    return m.astype(BF16)


def _dft_inverse_matrix(l, th):
    n = 2 * l
    k, part = _dft_index(l, th)
    t = jnp.arange(l, dtype=jnp.int32)[:, None]
    k, part = k[None, :], part[None, :]
    nyq = (k == 0) & (part == 1)
    kk = jnp.where(nyq, l, k)
    ang = ((kk * t) % n).astype(F32) * (2.0 * math.pi / n)
    weight = jnp.where((k == 0), 1.0 / n, 2.0 / n)
    m = jnp.where((part == 0) | nyq, jnp.cos(ang), -jnp.sin(ang)) * weight
    return m.astype(BF16)


def _filter_spectrum_body(a_ref, b_ref, hl_ref, o_ref, acc_ref):
    kk = pl.program_id(2)

    @pl.when(kk == 0)
    def _():
        acc_ref[...] = jnp.zeros_like(acc_ref)

    acc_ref[...] += jnp.dot(a_ref[...], b_ref[...], preferred_element_type=F32)

    @pl.when(kk == pl.num_programs(2) - 1)
    def _():
        o_ref[...] = acc_ref[...]

    @pl.when((kk == pl.num_programs(2) - 1) & (pl.program_id(0) == 1))
    def _():
        o_ref[0:1, :] = hl_ref[...]


def _filter_spectrum(fwd, ed, hl, th):
    m, k = fwd.shape
    c = ed.shape[1] // 2
    tn = _tile(c, 1024)
    tk = _tile(k, 512)
    nj = c // tn
    return pl.pallas_call(
        _filter_spectrum_body,
        grid=(m // th, nj, k // tk),
        in_specs=[
            pl.BlockSpec((th, tk), lambda i, j, kk: (i, kk)),
            pl.BlockSpec((tk, tn), lambda i, j, kk: (kk, (i % 2) * nj + j)),
            pl.BlockSpec((1, tn), lambda i, j, kk: (0, j)),
        ],
        out_specs=pl.BlockSpec((th, tn), lambda i, j, kk: (i, j)),
        out_shape=jax.ShapeDtypeStruct((m, c), F32),
        scratch_shapes=[pltpu.VMEM((th, tn), F32)],
        compiler_params=_cparams("parallel", "parallel", "arbitrary"),
        name="filter_spectrum",
    )(fwd, ed, hl)


def _dft_mul_body(a_ref, b_ref, h_ref, o_ref, acc_ref, *, th):
    kk = pl.program_id(2)

    @pl.when(kk == 0)
    def _():
        acc_ref[...] = jnp.zeros_like(acc_ref)

    acc_ref[...] += jnp.dot(a_ref[...], b_ref[...], preferred_element_type=F32)

    @pl.when(kk == pl.num_programs(2) - 1)
    def _():
        ur, ui = acc_ref[:th, :], acc_ref[th:, :]
        hr, hi = h_ref[:th, :], h_ref[th:, :]
        row = lax.broadcasted_iota(jnp.int32, (th, 1), 0)
        real_pair = (row == 0) & (pl.program_id(0) == 0)
        uihi = ui * hi
        o_ref[:th, :] = (ur * hr - jnp.where(real_pair, 0.0, uihi)).astype(o_ref.dtype)
        o_ref[th:, :] = jnp.where(real_pair, uihi, ur * hi + ui * hr).astype(o_ref.dtype)


def _dft_mul(fwd, zb, hspec, th):
    m, k = fwd.shape
    n = zb.shape[1]
    c = hspec.shape[1]
    tm = 2 * th
    tn = _tile(c, 1024)
    tk = _tile(k, 512)
    ncj = c // tn
    return pl.pallas_call(
        functools.partial(_dft_mul_body, th=th),
        grid=(m // tm, n // tn, k // tk),
        in_specs=[
            pl.BlockSpec((tm, tk), lambda i, j, kk: (i, kk)),
            pl.BlockSpec((tk, tn), lambda i, j, kk: (kk, j)),
            pl.BlockSpec((tm, tn), lambda i, j, kk: (i, j % ncj)),
        ],
        out_specs=pl.BlockSpec((tm, tn), lambda i, j, kk: (i, j)),
        out_shape=jax.ShapeDtypeStruct((m, n), BF16),
        scratch_shapes=[pltpu.VMEM((tm, tn), F32)],
        compiler_params=_cparams("parallel", "parallel", "arbitrary"),
        name="dft_mul",
    )(fwd, zb, hspec)


def _matmul_body(a_ref, b_ref, o_ref, acc_ref):
    kk = pl.program_id(2)

    @pl.when(kk == 0)
    def _():
        acc_ref[...] = jnp.zeros_like(acc_ref)

    acc_ref[...] += jnp.dot(a_ref[...], b_ref[...], preferred_element_type=F32)

    @pl.when(kk == pl.num_programs(2) - 1)
    def _():
        o_ref[...] = acc_ref[...]


def _dft_inverse(inv, v):
    m, k = inv.shape
    n = v.shape[1]
    tm = _tile(m, 1024)
    tn = _tile(n, 1024)
    tk = _tile(k, 512)
    return pl.pallas_call(
        _matmul_body,
        grid=(m // tm, n // tn, k // tk),
        in_specs=[
            pl.BlockSpec((tm, tk), lambda i, j, kk: (i, kk)),
            pl.BlockSpec((tk, tn), lambda i, j, kk: (kk, j)),
        ],
        out_specs=pl.BlockSpec((tm, tn), lambda i, j, kk: (i, j)),
        out_shape=jax.ShapeDtypeStruct((m, n), F32),
        scratch_shapes=[pltpu.VMEM((tm, tn), F32)],
        compiler_params=_cparams("parallel", "parallel", "arbitrary"),
        name="dft_inverse",
    )(inv, v)


def _conv_gate_body(p0_ref, p1_ref, p2_ref, w_ref, b_ref, x0_ref, z_ref, zb_ref, *, period, tc):
    tm = p0_ref.shape[0]
    pos = lax.broadcasted_iota(jnp.int32, (tm, 1), 0) % period
    first, last = pos == 0, pos == period - 1

    def conv(p, part):
        w = w_ref[:, part * tc:(part + 1) * tc]
        b = b_ref[:, part * tc:(part + 1) * tc]
        prev = jnp.where(first, 0.0, pltpu.roll(p, 1, axis=0))
        nxt = jnp.where(last, 0.0, pltpu.roll(p, tm - 1, axis=0))
        return prev * w[0:1, :] + p * w[1:2, :] + nxt * w[2:3, :] + b

    x0_ref[...] = conv(p0_ref[...], 0)
    z = conv(p1_ref[...], 1) * conv(p2_ref[...], 2)
    z_ref[...] = z
    zb_ref[...] = z.astype(BF16)


def _conv_gate(p, conv_w, conv_b, c, seq_len, period):
    r = p.shape[0]
    nb = r // seq_len
    tc = _tile(c, 256)
    ncj = c // tc
    tm = _tile(seq_len, 512)
    assert tm % period == 0
    spb = seq_len // tm
    w = conv_w.reshape(3, 3, ncj, tc).transpose(2, 0, 1, 3).reshape(ncj, 3, 3 * tc)
    b = conv_b.reshape(3, ncj, tc).transpose(1, 0, 2).reshape(ncj, 1, 3 * tc)
    col = lambda part: pl.BlockSpec((tm, tc), lambda i, j: (i, part * ncj + j))
    return pl.pallas_call(
        functools.partial(_conv_gate_body, period=period, tc=tc),
        grid=(r // tm, ncj),
        in_specs=[col(0), col(1), col(2),
                  pl.BlockSpec((None, 3, 3 * tc), lambda i, j: (j, 0, 0)),
                  pl.BlockSpec((None, 1, 3 * tc), lambda i, j: (j, 0, 0))],
        out_specs=[
            pl.BlockSpec((tm, tc), lambda i, j: (i, j)),
            pl.BlockSpec((tm, tc), lambda i, j: (i, j)),
            pl.BlockSpec((tm, tc), lambda i, j: (i % spb, (i // spb) * ncj + j)),
        ],
        out_shape=[jax.ShapeDtypeStruct((r, c), F32), jax.ShapeDtypeStruct((r, c), F32),
                   jax.ShapeDtypeStruct((seq_len, nb * c), BF16)],
        compiler_params=_cparams("parallel", "parallel"),
        name="conv_gate",
    )(p, p, p, w, b)


def _split_bf16(x):
    hi = x.astype(BF16)
    lo = (x - hi.astype(F32)).astype(BF16)
    return hi, lo


def _gla_body(qf_ref, kf_ref, vf_ref, lrf_ref, qb_ref, kb_ref, vb_ref, lrb_ref, gw_ref, gb_ref,
              s0f_ref, s0b_ref, of_ref, ob_ref, sf_ref, sb_ref, *, dk, dv, gsz):
    step = pl.program_id(1)
    ck = GLA_CHUNK
    tb = gsz * ck

    @pl.when(step == 0)
    def _():
        sf_ref[...] = s0f_ref[...]
        sb_ref[...] = s0b_ref[...]

    rows = lax.broadcasted_iota(jnp.int32, (tb, tb), 0)
    cols = lax.broadcasted_iota(jnp.int32, (tb, tb), 1)
    same_chunk = (rows // ck) == (cols // ck)
    crow = lax.broadcasted_iota(jnp.int32, (ck, ck), 0)
    ccol = lax.broadcasted_iota(jnp.int32, (ck, ck), 1)
    scale = dk ** -0.5

    def direction(q_ref, k_ref, v_ref, lr_ref, o_ref, s_ref, d):
        fwd = d == 0
        g = _log_sigmoid(jnp.dot(lr_ref[...].astype(BF16), gw_ref[d], preferred_element_type=F32)
                         + gb_ref[d]) / GATE_TAU
        tri = (same_chunk & ((cols <= rows) if fwd else (cols >= rows))).astype(BF16)
        g_hi, g_lo = _split_bf16(g)
        bcum = jnp.dot(tri, g_hi, preferred_element_type=F32) + jnp.dot(tri, g_lo, preferred_element_type=F32)
        keep = (ccol <= crow) if fwd else (ccol >= crow)
        order = range(gsz) if fwd else range(gsz - 1, -1, -1)
        for ci in order:
            r0 = ci * ck
            end = r0 + ck - 1 if fwd else r0
            for h in range(GLA_HEADS):
                kc = slice(h * dk, (h + 1) * dk)
                vc = slice(h * dv, (h + 1) * dv)
                b = bcum[r0:r0 + ck, kc]
                b_end = bcum[end:end + 1, kc]
                q = q_ref[r0:r0 + ck, kc] * scale
                k = k_ref[r0:r0 + ck, kc]
                v = v_ref[r0:r0 + ck, vc].astype(BF16)
                q_t = (q * jnp.exp(b)).astype(BF16)
                k_t = (k * jnp.exp(-b)).astype(BF16)
                k_d = (k * jnp.exp(b_end - b)).astype(BF16)
                att = lax.dot_general(q_t, k_t, (((1,), (1,)), ((), ())), preferred_element_type=F32)
                att = jnp.where(keep, att, 0.0).astype(BF16)
                s_t = s_ref[h]
                o = jnp.dot(att, v, preferred_element_type=F32) + \
                    lax.dot_general(q_t, s_t.astype(BF16), (((1,), (1,)), ((), ())), preferred_element_type=F32)
                o_ref[r0:r0 + ck, vc] = o
                upd = lax.dot_general(v, k_d, (((0,), (0,)), ((), ())), preferred_element_type=F32)
                s_ref[h] = jnp.exp(b_end) * s_t + upd

    direction(qf_ref, kf_ref, vf_ref, lrf_ref, of_ref, sf_ref, 0)
    direction(qb_ref, kb_ref, vb_ref, lrb_ref, ob_ref, sb_ref, 1)


def _gla_scan(p, lr, gate_w, gate_b, s0_f, s0_b, seq_len, cols):
    r = p.shape[0]
    nb = r // seq_len
    dkt = gate_w.shape[2]
    dk = dkt // GLA_HEADS
    dv = s0_f.shape[-2]
    dvt = dv * GLA_HEADS
    gsz = _tile(seq_len // GLA_CHUNK, 4)
    tb = gsz * GLA_CHUNK
    n = seq_len // tb
    q0, k0, v0 = cols
    fwd_row = lambda b, i: b * n + i
    bwd_row = lambda b, i: b * n + (n - 1 - i)

    def specs(row):
        return [
            pl.BlockSpec((tb, dkt), lambda b, i: (row(b, i), q0 // dkt)),
            pl.BlockSpec((tb, dkt), lambda b, i: (row(b, i), k0 // dkt)),
            pl.BlockSpec((tb, dvt), lambda b, i: (row(b, i), v0 // dvt)),
            pl.BlockSpec((tb, LANES), lambda b, i: (row(b, i), 0)),
        ]

    state_spec = pl.BlockSpec((None, GLA_HEADS, dv, dk), lambda b, i: (b, 0, 0, 0))
    return pl.pallas_call(
        functools.partial(_gla_body, dk=dk, dv=dv, gsz=gsz),
        grid=(nb, n),
        in_specs=specs(fwd_row) + specs(bwd_row) + [
            pl.BlockSpec((2, LANES, dkt), lambda b, i: (0, 0, 0)),
            pl.BlockSpec((2, 1, dkt), lambda b, i: (0, 0, 0)),
            state_spec, state_spec,
        ],
        out_specs=[
            pl.BlockSpec((tb, dvt), lambda b, i: (fwd_row(b, i), 0)),
            pl.BlockSpec((tb, dvt), lambda b, i: (bwd_row(b, i), 0)),
            state_spec, state_spec,
        ],
        out_shape=[jax.ShapeDtypeStruct((r, dvt), F32), jax.ShapeDtypeStruct((r, dvt), F32),
                   jax.ShapeDtypeStruct(s0_f.shape, F32), jax.ShapeDtypeStruct(s0_b.shape, F32)],
        compiler_params=_cparams("parallel", "arbitrary"),
        name="gla_scan",
    )(p, p, p, lr, p, p, p, lr, gate_w, gate_b.reshape(2, 1, dkt), s0_f, s0_b)


def _mix_out_body(x_ref, mod_ref, x0_ref, z_ref, r_ref, of_ref, ob_ref, rg_ref, gh_ref, gg_ref,
                  skip_ref, gn_ref, wh_ref, wg_ref, wo_ref, o_ref, *, d, dv):
    y_hy = x0_ref[...] * (r_ref[...] + skip_ref[...] * z_ref[...])
    o = of_ref[...] + ob_ref[...]
    tm = o.shape[0]
    rg = rg_ref[...]
    parts = []
    for h in range(GLA_HEADS):
        oh = o[:, h * dv:(h + 1) * dv]
        on = oh * lax.rsqrt(jnp.mean(oh * oh, axis=-1, keepdims=True) + EPS) * gn_ref[...]
        parts.append((on * _silu(rg[:, h * dv:(h + 1) * dv])).astype(BF16))
    y_gla = jnp.concatenate(parts, axis=-1)
    ph = jnp.dot(y_hy.astype(BF16), wh_ref[...], preferred_element_type=F32)
    pg = jnp.dot(y_gla, wg_ref[...], preferred_element_type=F32)
    merged = _sigmoid(gh_ref[...]) * ph + _sigmoid(gg_ref[...]) * pg
    out = jnp.dot(merged.astype(BF16), wo_ref[...], preferred_element_type=F32)
    g1 = mod_ref[:, 2 * d:3 * d]
    o_ref[...] = x_ref[...] + g1 * out


def _mix_out(x2, mod_l, x0, z, rconv, o_f, o_b, p, cols, skip, gn, w_bhy, w_bgla, w_o,
             seq_len, rows_per_cond, cond_base):
    r, d = x2.shape
    c = x0.shape[1]
    dvt = o_f.shape[1]
    dv = dvt // GLA_HEADS
    rg0, gh0, gg0 = cols
    tm = _tile(min(seq_len, rows_per_cond), 256)
    spb = seq_len // tm
    bpc = rows_per_cond // tm
    row = lambda i: (i, 0)
    full = lambda shape: pl.BlockSpec(shape, lambda i: (0, 0))
    return pl.pallas_call(
        functools.partial(_mix_out_body, d=d, dv=dv),
        grid=(r // tm,),
        in_specs=[
            pl.BlockSpec((tm, d), row),
            pl.BlockSpec((None, 1, N_MOD * d), lambda i: (cond_base + i // bpc, 0, 0)),
            pl.BlockSpec((tm, c), row),
            pl.BlockSpec((tm, c), row),
            pl.BlockSpec((tm, c), lambda i: (i % spb, i // spb)),
            pl.BlockSpec((tm, dvt), row),
            pl.BlockSpec((tm, dvt), row),
            pl.BlockSpec((tm, dvt), lambda i: (i, rg0 // dvt)),
            pl.BlockSpec((tm, d), lambda i: (i, gh0 // d)),
            pl.BlockSpec((tm, d), lambda i: (i, gg0 // d)),
            full((1, c)), full((1, dv)), full((c, d)), full((dvt, d)), full((d, d)),
        ],
        out_specs=pl.BlockSpec((tm, d), row),
        out_shape=jax.ShapeDtypeStruct((r, d), F32),
        compiler_params=_cparams("parallel"),
        name="mix_out",
    )(x2, mod_l, x0, z, rconv, o_f, o_b, p, p, p, skip, gn, w_bhy, w_bgla, w_o)


def _mlp_body(x_ref, mod_ref, g_ref, w1_ref, w2_ref, fg_ref, o_ref, h_scr, acc_scr, *, d, final_norm):
    f = pl.program_id(1)

    @pl.when(f == 0)
    def _():
        h_scr[...] = _modulated_norm(x_ref[...], g_ref[...], mod_ref[...], 3, 4, d).astype(BF16)
        acc_scr[...] = jnp.zeros_like(acc_scr)

    a = jnp.maximum(jnp.dot(h_scr[...], w1_ref[...], preferred_element_type=F32), 0.0)
    acc_scr[...] += jnp.dot((a * a).astype(BF16), w2_ref[...], preferred_element_type=F32)

    @pl.when(f == pl.num_programs(1) - 1)
    def _():
        g2 = mod_ref[:, 5 * d:6 * d]
        y = x_ref[...] + g2 * acc_scr[...]
        if final_norm:
            y = y * lax.rsqrt(jnp.mean(y * y, axis=-1, keepdims=True) + EPS) * fg_ref[...]
        o_ref[...] = y


def _mlp(x2, mod_l, norm_g, w1, w2, final_g, rows_per_cond, cond_base, final_norm):
    r, d = x2.shape
    dff = w1.shape[1]
    tm = _tile(min(r, rows_per_cond), 1024)
    tf = _tile(dff, 1024)
    bpc = rows_per_cond // tm
    return pl.pallas_call(
        functools.partial(_mlp_body, d=d, final_norm=final_norm),
        grid=(r // tm, dff // tf),
        in_specs=[
            pl.BlockSpec((tm, d), lambda i, f: (i, 0)),
            pl.BlockSpec((None, 1, N_MOD * d), lambda i, f: (cond_base + i // bpc, 0, 0)),
            pl.BlockSpec((1, d), lambda i, f: (0, 0)),
            pl.BlockSpec((d, tf), lambda i, f: (0, f)),
            pl.BlockSpec((tf, d), lambda i, f: (f, 0)),
            pl.BlockSpec((1, d), lambda i, f: (0, 0)),
        ],
        out_specs=pl.BlockSpec((tm, d), lambda i, f: (i, 0)),
        out_shape=jax.ShapeDtypeStruct((r, d), F32),
        scratch_shapes=[pltpu.VMEM((tm, d), BF16), pltpu.VMEM((tm, d), F32)],
        compiler_params=_cparams("parallel", "arbitrary"),
        name="mlp",
    )(x2, mod_l, norm_g, w1, w2, final_g)


def _hyena_long_conv(zb, ed, hl, fwd, inv, th):
    hspec = _filter_spectrum(fwd, ed, hl, th)
    return _dft_inverse(inv, _dft_mul(fwd, zb, hspec, th))


def kernel(x, c, ctx, c_ctx, ada_w, ada_b, norm1_g, norm2_g, w_in, hy_conv_w, hy_conv_b, hy_filt_w1, hy_filt_b1, hy_filt_w2, hy_filt_b2, hy_filt_w3, hy_filt_freq, hy_decay, hy_skip, gla_gate_w, gla_gate_b, gla_norm_g, w_branch_hy, w_branch_gla, w_out, mlp_w1, mlp_w2, final_g):
    nb, seq, d = x.shape
    lctx = ctx.shape[1]
    depth = ada_w.shape[0]
    chy = hy_decay.shape[-1]
    rank = gla_gate_w.shape[2]
    dkt = gla_gate_w.shape[3]
    dvt = w_branch_gla.shape[1]
    dk, dv = dkt // GLA_HEADS, dvt // GLA_HEADS
    assert nb + 1 <= COND_ROWS and 2 * rank <= LANES

    cond = jnp.zeros((COND_ROWS, d), F32).at[:nb].set(c).at[nb].set(c_ctx)
    mod = _ada_mod(cond, ada_w, ada_b).reshape(depth, COND_ROWS, 1, N_MOD * d)

    sizes = (3 * chy, dkt, dkt, dvt, dvt, rank, rank, d, d)
    offs = np.concatenate([[0], np.cumsum(sizes)])
    seg = lambda w, i: w[..., offs[i]:offs[i + 1]]
    w_main = jnp.concatenate([seg(w_in, i) for i in (0, 1, 2, 3, 4, 7, 8)], axis=-1).astype(BF16)
    w_lr = jnp.concatenate([seg(w_in, 5), seg(w_in, 6),
                            jnp.zeros((depth, d, LANES - 2 * rank), F32)], axis=-1).astype(BF16)
    q0 = 3 * chy
    k0 = q0 + dkt
    v0 = k0 + dkt
    rg0 = v0 + dvt
    gh0 = rg0 + dvt
    gg0 = gh0 + d

    gate_w_pad = jnp.zeros((depth, 2, LANES, dkt), F32)
    gate_w_pad = gate_w_pad.at[:, 0, :rank].set(gla_gate_w[:, 0]).at[:, 1, rank:2 * rank].set(gla_gate_w[:, 1])
    gate_w_pad = gate_w_pad.astype(BF16)

    w_bhy = w_branch_hy.astype(BF16)
    w_bgla = w_branch_gla.astype(BF16)
    w_o = w_out.astype(BF16)
    w1 = mlp_w1.astype(BF16)
    w2 = mlp_w2.astype(BF16)

    th_lat = _tile(seq, 512)
    th_ctx = _tile(lctx, 512)
    fwd_lat, inv_lat = _dft_forward_matrix(seq, th_lat), _dft_inverse_matrix(seq, th_lat)
    fwd_ctx, inv_ctx = _dft_forward_matrix(lctx, th_ctx), _dft_inverse_matrix(lctx, th_ctx)
    feat_lat, feat_ctx = _filter_features(seq), _filter_features(lctx)
    zero_state = jnp.zeros((nb, GLA_HEADS, dv, dk), F32)

    xs = x.reshape(nb * seq, d)
    cs = ctx.reshape(nb * lctx, d)
    for l in range(depth):
        last = l == depth - 1
        mod_l = mod[l]
        n1 = norm1_g[l].reshape(1, d)
        filt = (hy_filt_w1[l], hy_filt_b1[l], hy_filt_w2[l], hy_filt_b2[l], hy_filt_w3[l],
                hy_filt_freq[l], hy_decay[l])
        skip = hy_skip[l].reshape(1, chy)
        gn = gla_norm_g[l].reshape(1, dv)

        p_c, lr_c = _in_proj(cs, mod_l, n1, w_main[l], w_lr[l], nb * lctx, nb)
        of_c, ob_c, sf_c, sb_c = _gla_scan(p_c, lr_c, gate_w_pad[l], gla_gate_b[l], zero_state, zero_state,
                                           lctx, (q0, k0, v0))
        p_l, lr_l = _in_proj(xs, mod_l, n1, w_main[l], w_lr[l], seq, 0)
        of_l, ob_l, _, _ = _gla_scan(p_l, lr_l, gate_w_pad[l], gla_gate_b[l], sf_c, sb_c, seq, (q0, k0, v0))

        x0_l, z_l, zb_l = _conv_gate(p_l, hy_conv_w[l], hy_conv_b[l], chy, seq, GRID_W)
        ed_l, hl_l = _hyena_filter_parts(feat_lat, *filt)
        r_l = _hyena_long_conv(zb_l, ed_l, hl_l, fwd_lat, inv_lat, th_lat)
        xs = _mix_out(xs, mod_l, x0_l, z_l, r_l, of_l, ob_l, p_l, (rg0, gh0, gg0), skip, gn,
                      w_bhy[l], w_bgla[l], w_o[l], seq, seq, 0)
        if not last:
            x0_c, z_c, zb_c = _conv_gate(p_c, hy_conv_w[l], hy_conv_b[l], chy, lctx, lctx)
            ed_c, hl_c = _hyena_filter_parts(feat_ctx, *filt)
            r_c = _hyena_long_conv(zb_c, ed_c, hl_c, fwd_ctx, inv_ctx, th_ctx)
            cs = _mix_out(cs, mod_l, x0_c, z_c, r_c, of_c, ob_c, p_c, (rg0, gh0, gg0), skip, gn,
                          w_bhy[l], w_bgla[l], w_o[l], lctx, nb * lctx, nb)

        n2 = norm2_g[l].reshape(1, d)
        fg = final_g.reshape(1, d)
        xs = _mlp(xs, mod_l, n2, w1[l], w2[l], fg, seq, 0, last)
        if not last:
            cs = _mlp(cs, mod_l, n2, w1[l], w2[l], fg, nb * lctx, nb, False)
    return xs.reshape(nb, seq, d)
```

```python
import functools
import math

import numpy as np
import jax
import jax.numpy as jnp
from jax import lax
from jax.experimental import pallas as pl
from jax.experimental.pallas import tpu as pltpu

F32 = jnp.float32
BF16 = jnp.bfloat16

GRID_W = 64
N_BANDS = 16
GLA_HEADS = 4
GLA_CHUNK = 64
GATE_TAU = 16.0
EPS = 1e-6
N_MOD = 6

V7X_VMEM_BYTES = 64 * 1024 * 1024
VMEM_LIMIT_BYTES = V7X_VMEM_BYTES * 3 // 4
LANES = 128
COND_ROWS = 8


def _cparams(*sem):
    return pltpu.CompilerParams(dimension_semantics=sem, vmem_limit_bytes=VMEM_LIMIT_BYTES)


def _tile(n, pref):
    t = min(n, pref)
    while n % t:
        t //= 2
    return t


def _sigmoid(x):
    return 1.0 / (1.0 + jnp.exp(-x))


def _silu(x):
    return x * _sigmoid(x)


def _log_sigmoid(x):
    return jnp.minimum(x, 0.0) - jnp.log(1.0 + jnp.exp(-jnp.abs(x)))


def _modulated_norm(x, g, mod, shift_idx, scale_idx, d):
    y = x * lax.rsqrt(jnp.mean(x * x, axis=-1, keepdims=True) + EPS) * g
    shift = mod[:, shift_idx * d:(shift_idx + 1) * d]
    scale = mod[:, scale_idx * d:(scale_idx + 1) * d]
    return y * (1.0 + scale) + shift


def _ada_body(c_ref, w_ref, b_ref, o_ref):
    s = _silu(c_ref[...]).astype(BF16)
    o_ref[...] = jnp.dot(s, w_ref[...].astype(BF16), preferred_element_type=F32) + b_ref[...]


def _ada_mod(cond, ada_w, ada_b):
    depth, d, n = ada_w.shape
    tn = _tile(n, 1536)
    return pl.pallas_call(
        _ada_body,
        grid=(depth, n // tn),
        in_specs=[
            pl.BlockSpec((COND_ROWS, d), lambda l, j: (0, 0)),
            pl.BlockSpec((None, d, tn), lambda l, j: (l, 0, j)),
            pl.BlockSpec((None, 1, tn), lambda l, j: (l, 0, j)),
        ],
        out_specs=pl.BlockSpec((None, COND_ROWS, tn), lambda l, j: (l, 0, j)),
        out_shape=jax.ShapeDtypeStruct((depth, COND_ROWS, n), F32),
        compiler_params=_cparams("parallel", "parallel"),
        name="ada_mod",
    )(cond, ada_w, ada_b.reshape(depth, 1, n))


def _in_proj_body(x_ref, mod_ref, g_ref, w_ref, wlr_ref, o_ref, olr_ref, h_scr, *, d):
    @pl.when(pl.program_id(1) == 0)
    def _():
        h = _modulated_norm(x_ref[...], g_ref[...], mod_ref[...], 0, 1, d).astype(BF16)
        h_scr[...] = h
        olr_ref[...] = jnp.dot(h, wlr_ref[...], preferred_element_type=F32)

    o_ref[...] = jnp.dot(h_scr[...], w_ref[...], preferred_element_type=F32)


def _in_proj(x2, mod_l, norm_g, w_main, w_lr, rows_per_cond, cond_base):
    r, d = x2.shape
    n = w_main.shape[1]
    tm = _tile(min(r, rows_per_cond), 1024)
    tn = _tile(n, 1024)
    blocks_per_cond = rows_per_cond // tm
    return pl.pallas_call(
        functools.partial(_in_proj_body, d=d),
        grid=(r // tm, n // tn),
        in_specs=[
            pl.BlockSpec((tm, d), lambda i, j: (i, 0)),
            pl.BlockSpec((None, 1, N_MOD * d), lambda i, j: (cond_base + i // blocks_per_cond, 0, 0)),
            pl.BlockSpec((1, d), lambda i, j: (0, 0)),
            pl.BlockSpec((d, tn), lambda i, j: (0, j)),
            pl.BlockSpec((d, LANES), lambda i, j: (0, 0)),
        ],
        out_specs=[
            pl.BlockSpec((tm, tn), lambda i, j: (i, j)),
            pl.BlockSpec((tm, LANES), lambda i, j: (i, 0)),
        ],
        out_shape=[jax.ShapeDtypeStruct((r, n), F32), jax.ShapeDtypeStruct((r, LANES), F32)],
        scratch_shapes=[pltpu.VMEM((tm, d), BF16)],
        compiler_params=_cparams("parallel", "arbitrary"),
        name="in_proj",
    )(x2, mod_l, norm_g, w_main, w_lr)


def _filter_body(feat_ref, w1_ref, b1_ref, w2_ref, b2_ref, w3_ref, fr_ref, dec_ref, fb_ref, *, c, tl):
    i = pl.program_id(0)
    hp = lax.Precision.HIGHEST
    feat = feat_ref[...]
    fr = fr_ref[...]
    a = jnp.sin(fr * (jnp.dot(feat, w1_ref[...], precision=hp, preferred_element_type=F32) + b1_ref[...]))
    a = jnp.sin(fr * (jnp.dot(a, w2_ref[...], precision=hp, preferred_element_type=F32) + b2_ref[...]))
    hk = jnp.dot(a, w3_ref[...], precision=hp, preferred_element_type=F32)
    t = feat[:, 0:1]
    window = jnp.exp(-t * jnp.abs(dec_ref[...]))
    h_f = hk[:, :c] * window
    h_b = hk[:, c:] * window
    row = i * tl + lax.broadcasted_iota(jnp.int32, (tl, 1), 0)
    h_b = jnp.where(row == 0, 0.0, h_b)
    fb_ref[:, :c] = h_f.astype(BF16)
    fb_ref[:, c:] = h_b.astype(BF16)


def _pad_to(a, shape):
    return jnp.pad(a, [(0, s - n) for n, s in zip(a.shape, shape)])


def _hyena_filter_parts(feat, w1, b1, w2, b2, w3, freq, decay):
    l = feat.shape[0]
    c = decay.shape[-1]
    fe = fo = LANES
    assert feat.shape[1] <= fe and w1.shape[1] <= fo
    feat = _pad_to(feat, (l, fe))
    w1, w2, w3 = _pad_to(w1, (fe, fo)), _pad_to(w2, (fo, fo)), _pad_to(w3, (fo, 2 * c))
    b1, b2, freq = _pad_to(b1, (fo,)), _pad_to(b2, (fo,)), _pad_to(freq, (fo,))
    tl = _tile(l, 512)
    full = lambda shape: pl.BlockSpec(shape, lambda i: (0, 0))
    return pl.pallas_call(
        functools.partial(_filter_body, c=c, tl=tl),
        grid=(l // tl,),
        in_specs=[
            pl.BlockSpec((tl, fe), lambda i: (i, 0)),
            full((fe, fo)), full((1, fo)), full((fo, fo)), full((1, fo)), full((fo, 2 * c)),
            full((1, fo)), full((1, c)),
        ],
        out_specs=pl.BlockSpec((tl, 2 * c), lambda i: (i, 0)),
        out_shape=jax.ShapeDtypeStruct((l, 2 * c), BF16),
        compiler_params=_cparams("parallel"),
        name="hyena_filter",
    )(feat, w1, b1.reshape(1, fo), w2, b2.reshape(1, fo), w3, freq.reshape(1, fo), decay.reshape(1, c))


def _filter_features(l):
    t = jnp.linspace(0.0, 1.0, l, dtype=F32)[:, None]
    ang = (2.0 * math.pi / l) * jnp.arange(l, dtype=F32)[:, None] * \
        jnp.linspace(1e-4, N_BANDS - 1, N_BANDS, dtype=F32)[None, :]
    return jnp.concatenate([t, jnp.cos(ang), -jnp.sin(ang)], axis=-1)


DFT_BLOCK = 256
BF16_SUBLANES = 16


def _dft_plan(l):
    nb = min(DFT_BLOCK, l)
    nslab = l // nb
    grp = max(BF16_SUBLANES, LANES // nslab)
    assert l % nb == 0 and nb % grp == 0
    return nb, nslab, grp


def _dft_constants(l):
    nb, nslab, grp = _dft_plan(l)
    n = 2 * l
    nk1 = nslab + 1
    ngrp = nb // grp
    i32 = jnp.int32
    gi = jnp.arange(ngrp, dtype=i32)[:, None, None]
    row = jnp.arange(nk1 * 2 * grp, dtype=i32)[None, :, None]
    col = jnp.arange(nslab * grp, dtype=i32)[None, None, :]
    k1, part, a_out = row // (2 * grp), (row // grp) % 2, row % grp
    n1, a_in = col // grp, col % grp
    t = nb * n1 + grp * gi + a_in
    ang = ((t * k1) % n).astype(F32) * (2.0 * math.pi / n)
    f1 = jnp.where(a_out == a_in, jnp.where(part == 0, jnp.cos(ang), -jnp.sin(ang)), 0.0)
    weight = jnp.where((k1 == 0) | (k1 == nslab), 1.0 / n, 2.0 / n)
    f1i = jnp.swapaxes(f1 * weight, 1, 2)
    r2 = jnp.arange(nb, dtype=i32)
    th = ((r2[:, None] * r2[None, :]) % nb).astype(F32) * (2.0 * math.pi / nb)
    cs, sn = jnp.cos(th), jnp.sin(th)
    f2 = jnp.concatenate([jnp.concatenate([cs, sn], 1), jnp.concatenate([-sn, cs], 1)], 0)
    return f1.astype(BF16), f1i.astype(BF16), f2.astype(BF16), f2.T.astype(BF16)


def _unrolled_loop(n, width, fn):
    def body(i, carry):
        for j in range(width):
            fn(i * width + j)
        return carry

    if n >= width:
        lax.fori_loop(0, n // width, body, 0)
    for i in range(n - n % width, n):
        fn(i)


def _residue_pairs(nk1, fn):
    _unrolled_loop(nk1 // 2, 2, lambda i: fn((2 * i, 2 * i + 1)))
    if nk1 % 2:
        fn((nk1 - 1,))


def _residue_columns(t_scr, ks, nb):
    ct = t_scr.shape[-1]
    return jnp.concatenate([t_scr[k1].reshape(2 * nb, ct) for k1 in ks], axis=1)


def _dft_stage1(src_ref, f1_ref, t_scr, nb, nslab, grp):
    ct = src_ref.shape[-1]
    nk1 = nslab + 1

    def group(gi):
        r0 = pl.multiple_of(gi * grp, grp)
        data = src_ref[:, pl.ds(r0, grp), :].reshape(nslab * grp, ct)
        t = jnp.dot(f1_ref[gi], data, preferred_element_type=F32)
        t_scr[:, :, pl.ds(r0, grp), :] = t.astype(BF16).reshape(nk1, 2, grp, ct)

    _unrolled_loop(nb // grp, 4, group)


def _spectrum_body(fb_ref, f1_ref, f2_ref, h_ref, t_scr, *, nb, nslab, grp):
    s = pl.program_id(1)
    ct = fb_ref.shape[-1]
    _dft_stage1(fb_ref, f1_ref, t_scr, nb, nslab, grp)

    @pl.when(s == 0)
    def _():
        h_ref[...] = jnp.zeros_like(h_ref)

    imag_sign = jnp.where(s == 0, 1.0, -1.0)

    def residues(ks):
        x = jnp.dot(f2_ref[...], _residue_columns(t_scr, ks, nb), preferred_element_type=F32)
        for i, k1 in enumerate(ks):
            h_ref[k1, :nb, :] += x[:nb, i * ct:(i + 1) * ct]
            h_ref[k1, nb:, :] += imag_sign * x[nb:, i * ct:(i + 1) * ct]

    _residue_pairs(nslab + 1, residues)


def _filter_spectrum(fb, consts, l):
    nb, nslab, grp = _dft_plan(l)
    f1, _, f2, _ = consts
    c = fb.shape[1] // 2
    ct = _tile(c, 256)
    ncj = c // ct
    nk1 = nslab + 1
    const = lambda a: pl.BlockSpec(a.shape, lambda j, s: (0,) * a.ndim)
    return pl.pallas_call(
        functools.partial(_spectrum_body, nb=nb, nslab=nslab, grp=grp),
        grid=(ncj, 2),
        in_specs=[pl.BlockSpec((nslab, nb, ct), lambda j, s: (0, 0, s * ncj + j)), const(f1), const(f2)],
        out_specs=pl.BlockSpec((nk1, 2 * nb, ct), lambda j, s: (0, 0, j)),
        out_shape=jax.ShapeDtypeStruct((nk1, 2 * nb, c), F32),
        scratch_shapes=[pltpu.VMEM((nk1, 2, nb, ct), BF16)],
        compiler_params=_cparams("parallel", "arbitrary"),
        name="filter_spectrum",
    )(fb.reshape(nslab, nb, 2 * c), f1, f2)


def _long_conv_body(z_ref, h_ref, f1_ref, f1i_ref, f2_ref, f2i_ref, r_ref, t_scr, *, nb, nslab, grp):
    ct = z_ref.shape[-1]
    nk1 = nslab + 1
    _dft_stage1(z_ref, f1_ref, t_scr, nb, nslab, grp)

    def spectral(ks):
        x = jnp.dot(f2_ref[...], _residue_columns(t_scr, ks, nb), preferred_element_type=F32)
        xr, xi = x[:nb], x[nb:]
        hr = jnp.concatenate([h_ref[k1, :nb, :] for k1 in ks], axis=1)
        hi = jnp.concatenate([h_ref[k1, nb:, :] for k1 in ks], axis=1)
        y = jnp.concatenate([xr * hr - xi * hi, xr * hi + xi * hr], axis=0).astype(BF16)
        u = jnp.dot(f2i_ref[...], y, preferred_element_type=F32).astype(BF16)
        for i, k1 in enumerate(ks):
            t_scr[k1] = u[:, i * ct:(i + 1) * ct].reshape(2, nb, ct)

    _residue_pairs(nk1, spectral)

    def inverse1(gi):
        r0 = pl.multiple_of(gi * grp, grp)
        data = t_scr[:, :, pl.ds(r0, grp), :].reshape(nk1 * 2 * grp, ct)
        out = jnp.dot(f1i_ref[gi], data, preferred_element_type=F32)
        r_ref[:, pl.ds(r0, grp), :] = out.reshape(nslab, grp, ct)

    _unrolled_loop(nb // grp, 4, inverse1)


def _long_conv(zb, hspec, consts, l):
    nb, nslab, grp = _dft_plan(l)
    f1, f1i, f2, f2i = consts
    rows, c = zb.shape
    nbatch = rows // l
    ct = _tile(c, 256)
    nk1 = nslab + 1
    const = lambda a: pl.BlockSpec(a.shape, lambda j, b: (0,) * a.ndim)
    seq_spec = pl.BlockSpec((None, nslab, nb, ct), lambda j, b: (b, 0, 0, j))
    out = pl.pallas_call(
        functools.partial(_long_conv_body, nb=nb, nslab=nslab, grp=grp),
        grid=(c // ct, nbatch),
        in_specs=[seq_spec, pl.BlockSpec((nk1, 2 * nb, ct), lambda j, b: (0, 0, j)),
                  const(f1), const(f1i), const(f2), const(f2i)],
        out_specs=seq_spec,
        out_shape=jax.ShapeDtypeStruct((nbatch, nslab, nb, c), F32),
        scratch_shapes=[pltpu.VMEM((nk1, 2, nb, ct), BF16)],
        compiler_params=_cparams("parallel", "parallel"),
        name="long_conv",
    )(zb.reshape(nbatch, nslab, nb, c), hspec, f1, f1i, f2, f2i)
    return out.reshape(rows, c)


def _conv_gate_body(p0_ref, p1_ref, p2_ref, w_ref, b_ref, x0_ref, z_ref, zb_ref, *, period, tc):
    tm = p0_ref.shape[0]
    pos = lax.broadcasted_iota(jnp.int32, (tm, 1), 0) % period
    first, last = pos == 0, pos == period - 1

    def conv(p, part):
        w = w_ref[:, part * tc:(part + 1) * tc]
        b = b_ref[:, part * tc:(part + 1) * tc]
        prev = jnp.where(first, 0.0, pltpu.roll(p, 1, axis=0))
        nxt = jnp.where(last, 0.0, pltpu.roll(p, tm - 1, axis=0))
        return prev * w[0:1, :] + p * w[1:2, :] + nxt * w[2:3, :] + b

    x0_ref[...] = conv(p0_ref[...], 0)
    z = conv(p1_ref[...], 1) * conv(p2_ref[...], 2)
    z_ref[...] = z
    zb_ref[...] = z.astype(BF16)


def _conv_gate(p, conv_w, conv_b, c, seq_len, period):
    r = p.shape[0]
    tc = _tile(c, 256)
    ncj = c // tc
    tm = _tile(seq_len, 512)
    assert tm % period == 0
    w = conv_w.reshape(3, 3, ncj, tc).transpose(2, 0, 1, 3).reshape(ncj, 3, 3 * tc)
    b = conv_b.reshape(3, ncj, tc).transpose(1, 0, 2).reshape(ncj, 1, 3 * tc)
    col = lambda part: pl.BlockSpec((tm, tc), lambda i, j: (i, part * ncj + j))
    return pl.pallas_call(
        functools.partial(_conv_gate_body, period=period, tc=tc),
        grid=(r // tm, ncj),
        in_specs=[col(0), col(1), col(2),
                  pl.BlockSpec((None, 3, 3 * tc), lambda i, j: (j, 0, 0)),
                  pl.BlockSpec((None, 1, 3 * tc), lambda i, j: (j, 0, 0))],
        out_specs=[
            pl.BlockSpec((tm, tc), lambda i, j: (i, j)),
            pl.BlockSpec((tm, tc), lambda i, j: (i, j)),
            pl.BlockSpec((tm, tc), lambda i, j: (i, j)),
        ],
        out_shape=[jax.ShapeDtypeStruct((r, c), F32), jax.ShapeDtypeStruct((r, c), F32),
                   jax.ShapeDtypeStruct((r, c), BF16)],
        compiler_params=_cparams("parallel", "parallel"),
        name="conv_gate",
    )(p, p, p, w, b)


def _split_bf16(x):
    hi = x.astype(BF16)
    lo = (x - hi.astype(F32)).astype(BF16)
    return hi, lo


def _gla_body(qf_ref, kf_ref, vf_ref, lrf_ref, qb_ref, kb_ref, vb_ref, lrb_ref, gw_ref, gb_ref,
              s0f_ref, s0b_ref, of_ref, ob_ref, sf_ref, sb_ref, *, dk, dv, gsz):
    step = pl.program_id(1)
    ck = GLA_CHUNK
    tb = gsz * ck

    @pl.when(step == 0)
    def _():
        sf_ref[...] = s0f_ref[...]
        sb_ref[...] = s0b_ref[...]

    rows = lax.broadcasted_iota(jnp.int32, (tb, tb), 0)
    cols = lax.broadcasted_iota(jnp.int32, (tb, tb), 1)
    same_chunk = (rows // ck) == (cols // ck)
    crow = lax.broadcasted_iota(jnp.int32, (ck, ck), 0)
    ccol = lax.broadcasted_iota(jnp.int32, (ck, ck), 1)
    scale = dk ** -0.5

    def direction(q_ref, k_ref, v_ref, lr_ref, o_ref, s_ref, d):
        fwd = d == 0
        g = _log_sigmoid(jnp.dot(lr_ref[...].astype(BF16), gw_ref[d], preferred_element_type=F32)
                         + gb_ref[d]) / GATE_TAU
        tri = (same_chunk & ((cols <= rows) if fwd else (cols >= rows))).astype(BF16)
        g_hi, g_lo = _split_bf16(g)
        bcum = jnp.dot(tri, g_hi, preferred_element_type=F32) + jnp.dot(tri, g_lo, preferred_element_type=F32)
        keep = (ccol <= crow) if fwd else (ccol >= crow)
        order = range(gsz) if fwd else range(gsz - 1, -1, -1)
        for ci in order:
            r0 = ci * ck
            end = r0 + ck - 1 if fwd else r0
            for h in range(GLA_HEADS):
                kc = slice(h * dk, (h + 1) * dk)
                vc = slice(h * dv, (h + 1) * dv)
                b = bcum[r0:r0 + ck, kc]
                b_end = bcum[end:end + 1, kc]
                q = q_ref[r0:r0 + ck, kc] * scale
                k = k_ref[r0:r0 + ck, kc]
                v = v_ref[r0:r0 + ck, vc].astype(BF16)
                q_t = (q * jnp.exp(b)).astype(BF16)
                k_t = (k * jnp.exp(-b)).astype(BF16)
                k_d = (k * jnp.exp(b_end - b)).astype(BF16)
                att = lax.dot_general(q_t, k_t, (((1,), (1,)), ((), ())), preferred_element_type=F32)
                att = jnp.where(keep, att, 0.0).astype(BF16)
                s_t = s_ref[h]
                o = jnp.dot(att, v, preferred_element_type=F32) + \
                    lax.dot_general(q_t, s_t.astype(BF16), (((1,), (1,)), ((), ())), preferred_element_type=F32)
                o_ref[r0:r0 + ck, vc] = o
                upd = lax.dot_general(v, k_d, (((0,), (0,)), ((), ())), preferred_element_type=F32)
                s_ref[h] = jnp.exp(b_end) * s_t + upd

    direction(qf_ref, kf_ref, vf_ref, lrf_ref, of_ref, sf_ref, 0)
    direction(qb_ref, kb_ref, vb_ref, lrb_ref, ob_ref, sb_ref, 1)


def _gla_scan(p, lr, gate_w, gate_b, s0_f, s0_b, seq_len, cols):
    r = p.shape[0]
    nb = r // seq_len
    dkt = gate_w.shape[2]
    dk = dkt // GLA_HEADS
    dv = s0_f.shape[-2]
    dvt = dv * GLA_HEADS
    gsz = _tile(seq_len // GLA_CHUNK, 4)
    tb = gsz * GLA_CHUNK
    n = seq_len // tb
    q0, k0, v0 = cols
    fwd_row = lambda b, i: b * n + i
    bwd_row = lambda b, i: b * n + (n - 1 - i)

    def specs(row):
        return [
            pl.BlockSpec((tb, dkt), lambda b, i: (row(b, i), q0 // dkt)),
            pl.BlockSpec((tb, dkt), lambda b, i: (row(b, i), k0 // dkt)),
            pl.BlockSpec((tb, dvt), lambda b, i: (row(b, i), v0 // dvt)),
            pl.BlockSpec((tb, LANES), lambda b, i: (row(b, i), 0)),
        ]

    state_spec = pl.BlockSpec((None, GLA_HEADS, dv, dk), lambda b, i: (b, 0, 0, 0))
    return pl.pallas_call(
        functools.partial(_gla_body, dk=dk, dv=dv, gsz=gsz),
        grid=(nb, n),
        in_specs=specs(fwd_row) + specs(bwd_row) + [
            pl.BlockSpec((2, LANES, dkt), lambda b, i: (0, 0, 0)),
            pl.BlockSpec((2, 1, dkt), lambda b, i: (0, 0, 0)),
            state_spec, state_spec,
        ],
        out_specs=[
            pl.BlockSpec((tb, dvt), lambda b, i: (fwd_row(b, i), 0)),
            pl.BlockSpec((tb, dvt), lambda b, i: (bwd_row(b, i), 0)),
            state_spec, state_spec,
        ],
        out_shape=[jax.ShapeDtypeStruct((r, dvt), F32), jax.ShapeDtypeStruct((r, dvt), F32),
                   jax.ShapeDtypeStruct(s0_f.shape, F32), jax.ShapeDtypeStruct(s0_b.shape, F32)],
        compiler_params=_cparams("parallel", "arbitrary"),
        name="gla_scan",
    )(p, p, p, lr, p, p, p, lr, gate_w, gate_b.reshape(2, 1, dkt), s0_f, s0_b)


def _mix_out_body(x_ref, mod_ref, x0_ref, z_ref, r_ref, of_ref, ob_ref, rg_ref, gh_ref, gg_ref,
                  skip_ref, gn_ref, wh_ref, wg_ref, wo_ref, o_ref, *, d, dv):
    y_hy = x0_ref[...] * (r_ref[...] + skip_ref[...] * z_ref[...])
    o = of_ref[...] + ob_ref[...]
    tm = o.shape[0]
    rg = rg_ref[...]
    parts = []
    for h in range(GLA_HEADS):
        oh = o[:, h * dv:(h + 1) * dv]
        on = oh * lax.rsqrt(jnp.mean(oh * oh, axis=-1, keepdims=True) + EPS) * gn_ref[...]
        parts.append((on * _silu(rg[:, h * dv:(h + 1) * dv])).astype(BF16))
    y_gla = jnp.concatenate(parts, axis=-1)
    ph = jnp.dot(y_hy.astype(BF16), wh_ref[...], preferred_element_type=F32)
    pg = jnp.dot(y_gla, wg_ref[...], preferred_element_type=F32)
    merged = _sigmoid(gh_ref[...]) * ph + _sigmoid(gg_ref[...]) * pg
    out = jnp.dot(merged.astype(BF16), wo_ref[...], preferred_element_type=F32)
    g1 = mod_ref[:, 2 * d:3 * d]
    o_ref[...] = x_ref[...] + g1 * out


def _mix_out(x2, mod_l, x0, z, rconv, o_f, o_b, p, cols, skip, gn, w_bhy, w_bgla, w_o,
             rows_per_cond, cond_base):
    r, d = x2.shape
    c = x0.shape[1]
    dvt = o_f.shape[1]
    dv = dvt // GLA_HEADS
    rg0, gh0, gg0 = cols
    tm = _tile(min(r, rows_per_cond), 256)
    bpc = rows_per_cond // tm
    row = lambda i: (i, 0)
    full = lambda shape: pl.BlockSpec(shape, lambda i: (0, 0))
    return pl.pallas_call(
        functools.partial(_mix_out_body, d=d, dv=dv),
        grid=(r // tm,),
        in_specs=[
            pl.BlockSpec((tm, d), row),
            pl.BlockSpec((None, 1, N_MOD * d), lambda i: (cond_base + i // bpc, 0, 0)),
            pl.BlockSpec((tm, c), row),
            pl.BlockSpec((tm, c), row),
            pl.BlockSpec((tm, c), row),
            pl.BlockSpec((tm, dvt), row),
            pl.BlockSpec((tm, dvt), row),
            pl.BlockSpec((tm, dvt), lambda i: (i, rg0 // dvt)),
            pl.BlockSpec((tm, d), lambda i: (i, gh0 // d)),
            pl.BlockSpec((tm, d), lambda i: (i, gg0 // d)),
            full((1, c)), full((1, dv)), full((c, d)), full((dvt, d)), full((d, d)),
        ],
        out_specs=pl.BlockSpec((tm, d), row),
        out_shape=jax.ShapeDtypeStruct((r, d), F32),
        compiler_params=_cparams("parallel"),
        name="mix_out",
    )(x2, mod_l, x0, z, rconv, o_f, o_b, p, p, p, skip, gn, w_bhy, w_bgla, w_o)


def _mlp_body(x_ref, mod_ref, g_ref, w1_ref, w2_ref, fg_ref, o_ref, h_scr, acc_scr, *, d, final_norm):
    f = pl.program_id(1)

    @pl.when(f == 0)
    def _():
        h_scr[...] = _modulated_norm(x_ref[...], g_ref[...], mod_ref[...], 3, 4, d).astype(BF16)
        acc_scr[...] = jnp.zeros_like(acc_scr)

    a = jnp.maximum(jnp.dot(h_scr[...], w1_ref[...], preferred_element_type=F32), 0.0)
    acc_scr[...] += jnp.dot((a * a).astype(BF16), w2_ref[...], preferred_element_type=F32)

    @pl.when(f == pl.num_programs(1) - 1)
    def _():
        g2 = mod_ref[:, 5 * d:6 * d]
        y = x_ref[...] + g2 * acc_scr[...]
        if final_norm:
            y = y * lax.rsqrt(jnp.mean(y * y, axis=-1, keepdims=True) + EPS) * fg_ref[...]
        o_ref[...] = y


def _mlp(x2, mod_l, norm_g, w1, w2, final_g, rows_per_cond, cond_base, final_norm):
    r, d = x2.shape
    dff = w1.shape[1]
    tm = _tile(min(r, rows_per_cond), 1024)
    tf = _tile(dff, 1024)
    bpc = rows_per_cond // tm
    return pl.pallas_call(
        functools.partial(_mlp_body, d=d, final_norm=final_norm),
        grid=(r // tm, dff // tf),
        in_specs=[
            pl.BlockSpec((tm, d), lambda i, f: (i, 0)),
            pl.BlockSpec((None, 1, N_MOD * d), lambda i, f: (cond_base + i // bpc, 0, 0)),
            pl.BlockSpec((1, d), lambda i, f: (0, 0)),
            pl.BlockSpec((d, tf), lambda i, f: (0, f)),
            pl.BlockSpec((tf, d), lambda i, f: (f, 0)),
            pl.BlockSpec((1, d), lambda i, f: (0, 0)),
        ],
        out_specs=pl.BlockSpec((tm, d), lambda i, f: (i, 0)),
        out_shape=jax.ShapeDtypeStruct((r, d), F32),
        scratch_shapes=[pltpu.VMEM((tm, d), BF16), pltpu.VMEM((tm, d), F32)],
        compiler_params=_cparams("parallel", "arbitrary"),
        name="mlp",
    )(x2, mod_l, norm_g, w1, w2, final_g)


def kernel(x, c, ctx, c_ctx, ada_w, ada_b, norm1_g, norm2_g, w_in, hy_conv_w, hy_conv_b, hy_filt_w1, hy_filt_b1, hy_filt_w2, hy_filt_b2, hy_filt_w3, hy_filt_freq, hy_decay, hy_skip, gla_gate_w, gla_gate_b, gla_norm_g, w_branch_hy, w_branch_gla, w_out, mlp_w1, mlp_w2, final_g):
    nb, seq, d = x.shape
    lctx = ctx.shape[1]
    depth = ada_w.shape[0]
    chy = hy_decay.shape[-1]
    rank = gla_gate_w.shape[2]
    dkt = gla_gate_w.shape[3]
    dvt = w_branch_gla.shape[1]
    dk, dv = dkt // GLA_HEADS, dvt // GLA_HEADS
    assert nb + 1 <= COND_ROWS and 2 * rank <= LANES

    cond = jnp.zeros((COND_ROWS, d), F32).at[:nb].set(c).at[nb].set(c_ctx)
    mod = _ada_mod(cond, ada_w, ada_b).reshape(depth, COND_ROWS, 1, N_MOD * d)

    sizes = (3 * chy, dkt, dkt, dvt, dvt, rank, rank, d, d)
    offs = np.concatenate([[0], np.cumsum(sizes)])
    seg = lambda w, i: w[..., offs[i]:offs[i + 1]]
    w_main = jnp.concatenate([seg(w_in, i) for i in (0, 1, 2, 3, 4, 7, 8)], axis=-1).astype(BF16)
    w_lr = jnp.concatenate([seg(w_in, 5), seg(w_in, 6),
                            jnp.zeros((depth, d, LANES - 2 * rank), F32)], axis=-1).astype(BF16)
    q0 = 3 * chy
    k0 = q0 + dkt
    v0 = k0 + dkt
    rg0 = v0 + dvt
    gh0 = rg0 + dvt
    gg0 = gh0 + d

    gate_w_pad = jnp.zeros((depth, 2, LANES, dkt), F32)
    gate_w_pad = gate_w_pad.at[:, 0, :rank].set(gla_gate_w[:, 0]).at[:, 1, rank:2 * rank].set(gla_gate_w[:, 1])
    gate_w_pad = gate_w_pad.astype(BF16)

    w_bhy = w_branch_hy.astype(BF16)
    w_bgla = w_branch_gla.astype(BF16)
    w_o = w_out.astype(BF16)
    w1 = mlp_w1.astype(BF16)
    w2 = mlp_w2.astype(BF16)

    dft_lat, dft_ctx = _dft_constants(seq), _dft_constants(lctx)
    feat_lat, feat_ctx = _filter_features(seq), _filter_features(lctx)
    zero_state = jnp.zeros((nb, GLA_HEADS, dv, dk), F32)

    xs = x.reshape(nb * seq, d)
    cs = ctx.reshape(nb * lctx, d)
    for l in range(depth):
        last = l == depth - 1
        mod_l = mod[l]
        n1 = norm1_g[l].reshape(1, d)
        filt = (hy_filt_w1[l], hy_filt_b1[l], hy_filt_w2[l], hy_filt_b2[l], hy_filt_w3[l],
                hy_filt_freq[l], hy_decay[l])
        skip = hy_skip[l].reshape(1, chy)
        gn = gla_norm_g[l].reshape(1, dv)

        p_c, lr_c = _in_proj(cs, mod_l, n1, w_main[l], w_lr[l], nb * lctx, nb)
        of_c, ob_c, sf_c, sb_c = _gla_scan(p_c, lr_c, gate_w_pad[l], gla_gate_b[l], zero_state, zero_state,
                                           lctx, (q0, k0, v0))
        p_l, lr_l = _in_proj(xs, mod_l, n1, w_main[l], w_lr[l], seq, 0)
        of_l, ob_l, _, _ = _gla_scan(p_l, lr_l, gate_w_pad[l], gla_gate_b[l], sf_c, sb_c, seq, (q0, k0, v0))

        x0_l, z_l, zb_l = _conv_gate(p_l, hy_conv_w[l], hy_conv_b[l], chy, seq, GRID_W)
        h_l = _filter_spectrum(_hyena_filter_parts(feat_lat, *filt), dft_lat, seq)
        r_l = _long_conv(zb_l, h_l, dft_lat, seq)
        xs = _mix_out(xs, mod_l, x0_l, z_l, r_l, of_l, ob_l, p_l, (rg0, gh0, gg0), skip, gn,
                      w_bhy[l], w_bgla[l], w_o[l], seq, 0)
        if not last:
            x0_c, z_c, zb_c = _conv_gate(p_c, hy_conv_w[l], hy_conv_b[l], chy, lctx, lctx)
            h_c = _filter_spectrum(_hyena_filter_parts(feat_ctx, *filt), dft_ctx, lctx)
            r_c = _long_conv(zb_c, h_c, dft_ctx, lctx)
            cs = _mix_out(cs, mod_l, x0_c, z_c, r_c, of_c, ob_c, p_c, (rg0, gh0, gg0), skip, gn,
                          w_bhy[l], w_bgla[l], w_o[l], nb * lctx, nb)

        n2 = norm2_g[l].reshape(1, d)
        fg = final_g.reshape(1, d)
        xs = _mlp(xs, mod_l, n2, w1[l], w2[l], fg, seq, 0, last)
        if not last:
            cs = _mlp(cs, mod_l, n2, w1[l], w2[l], fg, nb * lctx, nb, False)
    return xs.reshape(nb, seq, d)
```

```python
import functools
import math

import numpy as np
import jax
import jax.numpy as jnp
from jax import lax
from jax.experimental import pallas as pl
from jax.experimental.pallas import tpu as pltpu

F32 = jnp.float32
BF16 = jnp.bfloat16

GRID_W = 64
N_BANDS = 16
GLA_HEADS = 4
GLA_CHUNK = 64
GATE_TAU = 16.0
EPS = 1e-6
N_MOD = 6

V7X_VMEM_BYTES = 64 * 1024 * 1024
VMEM_LIMIT_BYTES = V7X_VMEM_BYTES * 3 // 4
LANES = 128
SUBLANES = 8
COND_ROWS = SUBLANES


def _cparams(*sem):
    return pltpu.CompilerParams(dimension_semantics=sem, vmem_limit_bytes=VMEM_LIMIT_BYTES)


def _tile(n, pref):
    t = min(n, pref)
    while n % t:
        t //= 2
    return t


def _sigmoid(x):
    return 1.0 / (1.0 + jnp.exp(-x))


def _silu(x):
    return x * _sigmoid(x)


def _log_sigmoid(x):
    return jnp.minimum(x, 0.0) - jnp.log(1.0 + jnp.exp(-jnp.abs(x)))


def _modulated_norm(x, g, mod, shift_idx, scale_idx, d):
    y = x * lax.rsqrt(jnp.mean(x * x, axis=-1, keepdims=True) + EPS) * g
    shift = mod[:, shift_idx * d:(shift_idx + 1) * d]
    scale = mod[:, scale_idx * d:(scale_idx + 1) * d]
    return y * (1.0 + scale) + shift


def _ada_body(c_ref, w_ref, b_ref, o_ref):
    s = _silu(c_ref[...]).astype(BF16)
    o_ref[...] = jnp.dot(s, w_ref[...].astype(BF16), preferred_element_type=F32) + b_ref[...]


def _ada_mod(cond, ada_w, ada_b):
    depth, d, n = ada_w.shape
    tn = _tile(n, 1536)
    return pl.pallas_call(
        _ada_body,
        grid=(depth, n // tn),
        in_specs=[
            pl.BlockSpec((COND_ROWS, d), lambda l, j: (0, 0)),
            pl.BlockSpec((None, d, tn), lambda l, j: (l, 0, j)),
            pl.BlockSpec((None, 1, tn), lambda l, j: (l, 0, j)),
        ],
        out_specs=pl.BlockSpec((None, COND_ROWS, tn), lambda l, j: (l, 0, j)),
        out_shape=jax.ShapeDtypeStruct((depth, COND_ROWS, n), F32),
        compiler_params=_cparams("parallel", "parallel"),
        name="ada_mod",
    )(cond, ada_w, ada_b.reshape(depth, 1, n))


def _in_proj_body(x_ref, mod_ref, g_ref, w_ref, wlr_ref, o_ref, olr_ref, h_scr, *, d):
    @pl.when(pl.program_id(1) == 0)
    def _():
        h = _modulated_norm(x_ref[...], g_ref[...], mod_ref[...], 0, 1, d).astype(BF16)
        h_scr[...] = h
        olr_ref[...] = jnp.dot(h, wlr_ref[...], preferred_element_type=F32)

    o_ref[...] = jnp.dot(h_scr[...], w_ref[...], preferred_element_type=F32).astype(o_ref.dtype)


def _in_proj(x2, mod_l, norm_g, w_main, w_lr, rows_per_cond, cond_base):
    r, d = x2.shape
    n = w_main.shape[1]
    tm = _tile(min(r, rows_per_cond), 1024)
    tn = _tile(n, 1024)
    blocks_per_cond = rows_per_cond // tm
    return pl.pallas_call(
        functools.partial(_in_proj_body, d=d),
        grid=(r // tm, n // tn),
        in_specs=[
            pl.BlockSpec((tm, d), lambda i, j: (i, 0)),
            pl.BlockSpec((None, 1, N_MOD * d), lambda i, j: (cond_base + i // blocks_per_cond, 0, 0)),
            pl.BlockSpec((1, d), lambda i, j: (0, 0)),
            pl.BlockSpec((d, tn), lambda i, j: (0, j)),
            pl.BlockSpec((d, LANES), lambda i, j: (0, 0)),
        ],
        out_specs=[
            pl.BlockSpec((tm, tn), lambda i, j: (i, j)),
            pl.BlockSpec((tm, LANES), lambda i, j: (i, 0)),
        ],
        out_shape=[jax.ShapeDtypeStruct((r, n), BF16), jax.ShapeDtypeStruct((r, LANES), F32)],
        scratch_shapes=[pltpu.VMEM((tm, d), BF16)],
        compiler_params=_cparams("parallel", "arbitrary"),
        name="in_proj",
    )(x2, mod_l, norm_g, w_main, w_lr)


def _hyena_proj_body(x_ref, mod_ref, g_ref, w_ref, cw_ref, cb_ref, x0_ref, z_ref, h_scr, *, d, period, tc):
    @pl.when(pl.program_id(1) == 0)
    def _():
        h_scr[...] = _modulated_norm(x_ref[...], g_ref[...], mod_ref[...], 0, 1, d).astype(BF16)

    tm = x_ref.shape[0]
    pos = lax.broadcasted_iota(jnp.int32, (tm, 1), 0) % period
    first, last = pos == 0, pos == period - 1

    p_all = jnp.dot(h_scr[...], w_ref[...], preferred_element_type=F32)

    def conv(part):
        cols = slice(part * tc, (part + 1) * tc)
        p = p_all[:, cols]
        prev = jnp.where(first, 0.0, pltpu.roll(p, 1, axis=0))
        nxt = jnp.where(last, 0.0, pltpu.roll(p, tm - 1, axis=0))
        return prev * cw_ref[0:1, cols] + p * cw_ref[1:2, cols] + nxt * cw_ref[2:3, cols] + cb_ref[:, cols]

    x0_ref[...] = conv(0).astype(BF16)
    z_ref[...] = (conv(1) * conv(2)).astype(BF16)


def _hyena_proj(x2, mod_l, norm_g, w_hy, conv_w, conv_b, rows_per_cond, cond_base, period):
    r, d = x2.shape
    ncj, _, tc3 = w_hy.shape
    tc = tc3 // 3
    c = ncj * tc
    tm = _tile(min(r, rows_per_cond), 1024)
    assert tm % period == 0
    blocks_per_cond = rows_per_cond // tm
    cw = conv_w.reshape(3, 3, ncj, tc).transpose(2, 0, 1, 3).reshape(ncj, 3, tc3)
    cb = conv_b.reshape(3, ncj, tc).transpose(1, 0, 2).reshape(ncj, 1, tc3)
    return pl.pallas_call(
        functools.partial(_hyena_proj_body, d=d, period=period, tc=tc),
        grid=(r // tm, ncj),
        in_specs=[
            pl.BlockSpec((tm, d), lambda i, j: (i, 0)),
            pl.BlockSpec((None, 1, N_MOD * d), lambda i, j: (cond_base + i // blocks_per_cond, 0, 0)),
            pl.BlockSpec((1, d), lambda i, j: (0, 0)),
            pl.BlockSpec((None, d, tc3), lambda i, j: (j, 0, 0)),
            pl.BlockSpec((None, 3, tc3), lambda i, j: (j, 0, 0)),
            pl.BlockSpec((None, 1, tc3), lambda i, j: (j, 0, 0)),
        ],
        out_specs=[pl.BlockSpec((tm, tc), lambda i, j: (i, j)), pl.BlockSpec((tm, tc), lambda i, j: (i, j))],
        out_shape=[jax.ShapeDtypeStruct((r, c), BF16), jax.ShapeDtypeStruct((r, c), BF16)],
        scratch_shapes=[pltpu.VMEM((tm, d), BF16)],
        compiler_params=_cparams("parallel", "arbitrary"),
        name="hyena_proj",
    )(x2, mod_l, norm_g, w_hy, cw, cb)


def _filter_body(feat_ref, w1_ref, b1_ref, w2_ref, b2_ref, w3_ref, fr_ref, dec_ref, fb_ref, *, c, tl):
    i = pl.program_id(0)
    hp = lax.Precision.HIGHEST
    feat = feat_ref[...]
    fr = fr_ref[...]
    a = jnp.sin(fr * (jnp.dot(feat, w1_ref[...], precision=hp, preferred_element_type=F32) + b1_ref[...]))
    a = jnp.sin(fr * (jnp.dot(a, w2_ref[...], precision=hp, preferred_element_type=F32) + b2_ref[...]))
    hk = jnp.dot(a, w3_ref[...], precision=hp, preferred_element_type=F32)
    t = feat[:, 0:1]
    window = jnp.exp(-t * jnp.abs(dec_ref[...]))
    h_f = hk[:, :c] * window
    h_b = hk[:, c:] * window
    row = i * tl + lax.broadcasted_iota(jnp.int32, (tl, 1), 0)
    h_b = jnp.where(row == 0, 0.0, h_b)
    fb_ref[:, :c] = h_f.astype(BF16)
    fb_ref[:, c:] = h_b.astype(BF16)


def _pad_to(a, shape):
    return jnp.pad(a, [(0, s - n) for n, s in zip(a.shape, shape)])


def _hyena_filter_parts(feat, w1, b1, w2, b2, w3, freq, decay):
    l = feat.shape[0]
    c = decay.shape[-1]
    fe = fo = LANES
    assert feat.shape[1] <= fe and w1.shape[1] <= fo
    feat = _pad_to(feat, (l, fe))
    w1, w2, w3 = _pad_to(w1, (fe, fo)), _pad_to(w2, (fo, fo)), _pad_to(w3, (fo, 2 * c))
    b1, b2, freq = _pad_to(b1, (fo,)), _pad_to(b2, (fo,)), _pad_to(freq, (fo,))
    tl = _tile(l, 512)
    full = lambda shape: pl.BlockSpec(shape, lambda i: (0, 0))
    return pl.pallas_call(
        functools.partial(_filter_body, c=c, tl=tl),
        grid=(l // tl,),
        in_specs=[
            pl.BlockSpec((tl, fe), lambda i: (i, 0)),
            full((fe, fo)), full((1, fo)), full((fo, fo)), full((1, fo)), full((fo, 2 * c)),
            full((1, fo)), full((1, c)),
        ],
        out_specs=pl.BlockSpec((tl, 2 * c), lambda i: (i, 0)),
        out_shape=jax.ShapeDtypeStruct((l, 2 * c), BF16),
        compiler_params=_cparams("parallel"),
        name="hyena_filter",
    )(feat, w1, b1.reshape(1, fo), w2, b2.reshape(1, fo), w3, freq.reshape(1, fo), decay.reshape(1, c))


def _filter_features(l):
    t = jnp.linspace(0.0, 1.0, l, dtype=F32)[:, None]
    ang = (2.0 * math.pi / l) * jnp.arange(l, dtype=F32)[:, None] * \
        jnp.linspace(1e-4, N_BANDS - 1, N_BANDS, dtype=F32)[None, :]
    return jnp.concatenate([t, jnp.cos(ang), -jnp.sin(ang)], axis=-1)


DFT_BLOCK = 256
BF16_SUBLANES = 16


def _dft_plan(l):
    nb = min(DFT_BLOCK, l)
    nslab = l // nb
    grp = max(BF16_SUBLANES, LANES // nslab)
    assert l % nb == 0 and nb % grp == 0
    return nb, nslab, grp


def _dft_constants(l):
    nb, nslab, grp = _dft_plan(l)
    n = 2 * l
    nk1 = nslab + 1
    ngrp = nb // grp
    i32 = jnp.int32
    gi = jnp.arange(ngrp, dtype=i32)[:, None, None]
    row = jnp.arange(nk1 * 2 * grp, dtype=i32)[None, :, None]
    col = jnp.arange(nslab * grp, dtype=i32)[None, None, :]
    k1, part, a_out = row // (2 * grp), (row // grp) % 2, row % grp
    n1, a_in = col // grp, col % grp
    t = nb * n1 + grp * gi + a_in
    ang = ((t * k1) % n).astype(F32) * (2.0 * math.pi / n)
    f1 = jnp.where(a_out == a_in, jnp.where(part == 0, jnp.cos(ang), -jnp.sin(ang)), 0.0)
    weight = jnp.where((k1 == 0) | (k1 == nslab), 1.0 / n, 2.0 / n)
    f1i = jnp.swapaxes(f1 * weight, 1, 2)
    r2 = jnp.arange(nb, dtype=i32)
    th = ((r2[:, None] * r2[None, :]) % nb).astype(F32) * (2.0 * math.pi / nb)
    cs, sn = jnp.cos(th), jnp.sin(th)
    f2 = jnp.concatenate([jnp.concatenate([cs, sn], 1), jnp.concatenate([-sn, cs], 1)], 0)
    return f1.astype(BF16), f1i.astype(BF16), f2.astype(BF16), f2.T.astype(BF16)


def _unrolled_loop(n, width, fn):
    def body(i, carry):
        for j in range(width):
            fn(i * width + j)
        return carry

    if n >= width:
        lax.fori_loop(0, n // width, body, 0)
    for i in range(n - n % width, n):
        fn(i)


def _residue_pairs(nk1, fn):
    _unrolled_loop(nk1 // 2, 2, lambda i: fn((2 * i, 2 * i + 1)))
    if nk1 % 2:
        fn((nk1 - 1,))


def _residue_columns(t_scr, ks, nb):
    ct = t_scr.shape[-1]
    return jnp.concatenate([t_scr[k1].reshape(2 * nb, ct) for k1 in ks], axis=1)


def _dft_stage1(src_ref, f1_ref, t_scr, nb, nslab, grp):
    ct = src_ref.shape[-1]
    nk1 = nslab + 1

    def group(gi):
        r0 = pl.multiple_of(gi * grp, grp)
        data = src_ref[:, pl.ds(r0, grp), :].reshape(nslab * grp, ct)
        t = jnp.dot(f1_ref[gi], data, preferred_element_type=F32)
        t_scr[:, :, pl.ds(r0, grp), :] = t.astype(BF16).reshape(nk1, 2, grp, ct)

    _unrolled_loop(nb // grp, 4, group)


def _spectrum_body(fb_ref, f1_ref, f2_ref, h_ref, t_scr, *, nb, nslab, grp):
    s = pl.program_id(1)
    ct = fb_ref.shape[-1]
    _dft_stage1(fb_ref, f1_ref, t_scr, nb, nslab, grp)

    @pl.when(s == 0)
    def _():
        h_ref[...] = jnp.zeros_like(h_ref)

    imag_sign = jnp.where(s == 0, 1.0, -1.0)

    def residues(ks):
        x = jnp.dot(f2_ref[...], _residue_columns(t_scr, ks, nb), preferred_element_type=F32)
        for i, k1 in enumerate(ks):
            h_ref[k1, :nb, :] += x[:nb, i * ct:(i + 1) * ct]
            h_ref[k1, nb:, :] += imag_sign * x[nb:, i * ct:(i + 1) * ct]

    _residue_pairs(nslab + 1, residues)


def _filter_spectrum(fb, consts, l):
    nb, nslab, grp = _dft_plan(l)
    f1, _, f2, _ = consts
    c = fb.shape[1] // 2
    ct = _tile(c, 256)
    ncj = c // ct
    nk1 = nslab + 1
    const = lambda a: pl.BlockSpec(a.shape, lambda j, s: (0,) * a.ndim)
    return pl.pallas_call(
        functools.partial(_spectrum_body, nb=nb, nslab=nslab, grp=grp),
        grid=(ncj, 2),
        in_specs=[pl.BlockSpec((nslab, nb, ct), lambda j, s: (0, 0, s * ncj + j)), const(f1), const(f2)],
        out_specs=pl.BlockSpec((nk1, 2 * nb, ct), lambda j, s: (0, 0, j)),
        out_shape=jax.ShapeDtypeStruct((nk1, 2 * nb, c), F32),
        scratch_shapes=[pltpu.VMEM((nk1, 2, nb, ct), BF16)],
        compiler_params=_cparams("parallel", "arbitrary"),
        name="filter_spectrum",
    )(fb.reshape(nslab, nb, 2 * c), f1, f2)


def _long_conv_body(z_ref, h_ref, f1_ref, f1i_ref, f2_ref, f2i_ref, r_ref, t_scr, *, nb, nslab, grp):
    ct = z_ref.shape[-1]
    nk1 = nslab + 1
    _dft_stage1(z_ref, f1_ref, t_scr, nb, nslab, grp)

    def spectral(ks):
        x = jnp.dot(f2_ref[...], _residue_columns(t_scr, ks, nb), preferred_element_type=F32)
        xr, xi = x[:nb], x[nb:]
        hr = jnp.concatenate([h_ref[k1, :nb, :] for k1 in ks], axis=1)
        hi = jnp.concatenate([h_ref[k1, nb:, :] for k1 in ks], axis=1)
        y = jnp.concatenate([xr * hr - xi * hi, xr * hi + xi * hr], axis=0).astype(BF16)
        u = jnp.dot(f2i_ref[...], y, preferred_element_type=F32).astype(BF16)
        for i, k1 in enumerate(ks):
            t_scr[k1] = u[:, i * ct:(i + 1) * ct].reshape(2, nb, ct)

    _residue_pairs(nk1, spectral)

    def inverse1(gi):
        r0 = pl.multiple_of(gi * grp, grp)
        data = t_scr[:, :, pl.ds(r0, grp), :].reshape(nk1 * 2 * grp, ct)
        out = jnp.dot(f1i_ref[gi], data, preferred_element_type=F32)
        r_ref[:, pl.ds(r0, grp), :] = out.astype(r_ref.dtype).reshape(nslab, grp, ct)

    _unrolled_loop(nb // grp, 4, inverse1)


def _long_conv(zb, hspec, consts, l):
    nb, nslab, grp = _dft_plan(l)
    f1, f1i, f2, f2i = consts
    rows, c = zb.shape
    nbatch = rows // l
    ct = _tile(c, 256)
    nk1 = nslab + 1
    const = lambda a: pl.BlockSpec(a.shape, lambda j, b: (0,) * a.ndim)
    seq_spec = pl.BlockSpec((None, nslab, nb, ct), lambda j, b: (b, 0, 0, j))
    out = pl.pallas_call(
        functools.partial(_long_conv_body, nb=nb, nslab=nslab, grp=grp),
        grid=(c // ct, nbatch),
        in_specs=[seq_spec, pl.BlockSpec((nk1, 2 * nb, ct), lambda j, b: (0, 0, j)),
                  const(f1), const(f1i), const(f2), const(f2i)],
        out_specs=seq_spec,
        out_shape=jax.ShapeDtypeStruct((nbatch, nslab, nb, c), BF16),
        scratch_shapes=[pltpu.VMEM((nk1, 2, nb, ct), BF16)],
        compiler_params=_cparams("parallel", "parallel"),
        name="long_conv",
    )(zb.reshape(nbatch, nslab, nb, c), hspec, f1, f1i, f2, f2i)
    return out.reshape(rows, c)


def _split_bf16(x):
    hi = x.astype(BF16)
    lo = (x - hi.astype(F32)).astype(BF16)
    return hi, lo


def _gla_body(qf_ref, kf_ref, vf_ref, lrf_ref, qb_ref, kb_ref, vb_ref, lrb_ref, gw_ref, gb_ref,
              s0f_ref, s0b_ref, of_ref, ob_ref, sf_ref, sb_ref, *, dk, dv, gsz):
    step = pl.program_id(1)
    ck = GLA_CHUNK
    tb = gsz * ck

    @pl.when(step == 0)
    def _():
        sf_ref[...] = s0f_ref[...]
        sb_ref[...] = s0b_ref[...]

    rows = lax.broadcasted_iota(jnp.int32, (tb, tb), 0)
    cols = lax.broadcasted_iota(jnp.int32, (tb, tb), 1)
    same_chunk = (rows // ck) == (cols // ck)
    scale = dk ** -0.5

    def direction(q_ref, k_ref, v_ref, lr_ref, o_ref, s_ref, d):
        fwd = d == 0
        g = _log_sigmoid(jnp.dot(lr_ref[...].astype(BF16), gw_ref[d], preferred_element_type=F32)
                         + gb_ref[d]) / GATE_TAU
        keep = same_chunk & ((cols <= rows) if fwd else (cols >= rows))
        tri = keep.astype(BF16)
        g_hi, g_lo = _split_bf16(g)
        bcum = jnp.dot(tri, g_hi, preferred_element_type=F32) + jnp.dot(tri, g_lo, preferred_element_type=F32)
        ends = [bcum[(c * ck + ck - 1 if fwd else c * ck):(c * ck + ck if fwd else c * ck + 1), :] for c in range(gsz)]
        b_end = jnp.concatenate([jnp.broadcast_to(e, (ck, e.shape[1])) for e in ends], axis=0)
        pad = [jnp.zeros_like(ends[0])] * (SUBLANES - gsz)
        decay_t = jnp.exp(jnp.concatenate(ends + pad, axis=0)).T
        q = q_ref[...].astype(F32) * scale
        k = k_ref[...].astype(F32)
        q_t = (q * jnp.exp(bcum)).astype(BF16)
        k_t = (k * jnp.exp(-bcum)).astype(BF16)
        k_d = (k * jnp.exp(b_end - bcum)).astype(BF16)
        order = range(gsz) if fwd else range(gsz - 1, -1, -1)
        for h in range(GLA_HEADS):
            kc = slice(h * dk, (h + 1) * dk)
            vc = slice(h * dv, (h + 1) * dv)
            v = v_ref[:, vc]
            att = lax.dot_general(q_t[:, kc], k_t[:, kc], (((1,), (1,)), ((), ())), preferred_element_type=F32)
            o_intra = jnp.dot(jnp.where(keep, att, 0.0).astype(BF16), v, preferred_element_type=F32)
            s = s_ref[h]
            for c in order:
                rs = slice(c * ck, (c + 1) * ck)
                o = o_intra[rs] + jnp.dot(q_t[rs, kc], s.astype(BF16), preferred_element_type=F32)
                o_ref[rs, vc] = o.astype(o_ref.dtype)
                upd = lax.dot_general(k_d[rs, kc], v[rs], (((0,), (0,)), ((), ())), preferred_element_type=F32)
                s = decay_t[kc, c:c + 1] * s + upd
            s_ref[h] = s

    direction(qf_ref, kf_ref, vf_ref, lrf_ref, of_ref, sf_ref, 0)
    direction(qb_ref, kb_ref, vb_ref, lrb_ref, ob_ref, sb_ref, 1)


def _gla_scan(p, lr, gate_w, gate_b, s0_f, s0_b, seq_len, cols):
    r = p.shape[0]
    nb = r // seq_len
    dkt = gate_w.shape[2]
    dk = dkt // GLA_HEADS
    dv = s0_f.shape[-1]
    dvt = dv * GLA_HEADS
    gsz = _tile(seq_len // GLA_CHUNK, 4)
    assert gsz <= SUBLANES
    tb = gsz * GLA_CHUNK
    n = seq_len // tb
    q0, k0, v0 = cols
    fwd_row = lambda b, i: b * n + i
    bwd_row = lambda b, i: b * n + (n - 1 - i)

    def specs(row):
        return [
            pl.BlockSpec((tb, dkt), lambda b, i: (row(b, i), q0 // dkt)),
            pl.BlockSpec((tb, dkt), lambda b, i: (row(b, i), k0 // dkt)),
            pl.BlockSpec((tb, dvt), lambda b, i: (row(b, i), v0 // dvt)),
            pl.BlockSpec((tb, LANES), lambda b, i: (row(b, i), 0)),
        ]

    state_spec = pl.BlockSpec((None, GLA_HEADS, dk, dv), lambda b, i: (b, 0, 0, 0))
    return pl.pallas_call(
        functools.partial(_gla_body, dk=dk, dv=dv, gsz=gsz),
        grid=(nb, n),
        in_specs=specs(fwd_row) + specs(bwd_row) + [
            pl.BlockSpec((2, LANES, dkt), lambda b, i: (0, 0, 0)),
            pl.BlockSpec((2, 1, dkt), lambda b, i: (0, 0, 0)),
            state_spec, state_spec,
        ],
        out_specs=[
            pl.BlockSpec((tb, dvt), lambda b, i: (fwd_row(b, i), 0)),
            pl.BlockSpec((tb, dvt), lambda b, i: (bwd_row(b, i), 0)),
            state_spec, state_spec,
        ],
        out_shape=[jax.ShapeDtypeStruct((r, dvt), BF16), jax.ShapeDtypeStruct((r, dvt), BF16),
                   jax.ShapeDtypeStruct(s0_f.shape, F32), jax.ShapeDtypeStruct(s0_b.shape, F32)],
        compiler_params=_cparams("parallel", "arbitrary"),
        name="gla_scan",
    )(p, p, p, lr, p, p, p, lr, gate_w, gate_b.reshape(2, 1, dkt), s0_f, s0_b)


def _mix_out_body(x_ref, mod_ref, x0_ref, z_ref, r_ref, of_ref, ob_ref, rg_ref, gh_ref, gg_ref,
                  skip_ref, gn_ref, wh_ref, wg_ref, wo_ref, o_ref, *, d, dv):
    f32 = lambda ref: ref[...].astype(F32)
    y_hy = f32(x0_ref) * (f32(r_ref) + skip_ref[...] * f32(z_ref))
    o = f32(of_ref) + f32(ob_ref)
    rg = f32(rg_ref)
    parts = []
    for h in range(GLA_HEADS):
        oh = o[:, h * dv:(h + 1) * dv]
        on = oh * lax.rsqrt(jnp.mean(oh * oh, axis=-1, keepdims=True) + EPS) * gn_ref[...]
        parts.append((on * _silu(rg[:, h * dv:(h + 1) * dv])).astype(BF16))
    y_gla = jnp.concatenate(parts, axis=-1)
    ph = jnp.dot(y_hy.astype(BF16), wh_ref[...], preferred_element_type=F32)
    pg = jnp.dot(y_gla, wg_ref[...], preferred_element_type=F32)
    merged = _sigmoid(f32(gh_ref)) * ph + _sigmoid(f32(gg_ref)) * pg
    out = jnp.dot(merged.astype(BF16), wo_ref[...], preferred_element_type=F32)
    g1 = mod_ref[:, 2 * d:3 * d]
    o_ref[...] = x_ref[...] + g1 * out


def _mix_out(x2, mod_l, x0, z, rconv, o_f, o_b, p, cols, skip, gn, w_bhy, w_bgla, w_o,
             rows_per_cond, cond_base):
    r, d = x2.shape
    c = x0.shape[1]
    dvt = o_f.shape[1]
    dv = dvt // GLA_HEADS
    rg0, gh0, gg0 = cols
    tm = _tile(min(r, rows_per_cond), 256)
    bpc = rows_per_cond // tm
    row = lambda i: (i, 0)
    full = lambda shape: pl.BlockSpec(shape, lambda i: (0, 0))
    return pl.pallas_call(
        functools.partial(_mix_out_body, d=d, dv=dv),
        grid=(r // tm,),
        in_specs=[
            pl.BlockSpec((tm, d), row),
            pl.BlockSpec((None, 1, N_MOD * d), lambda i: (cond_base + i // bpc, 0, 0)),
            pl.BlockSpec((tm, c), row),
            pl.BlockSpec((tm, c), row),
            pl.BlockSpec((tm, c), row),
            pl.BlockSpec((tm, dvt), row),
            pl.BlockSpec((tm, dvt), row),
            pl.BlockSpec((tm, dvt), lambda i: (i, rg0 // dvt)),
            pl.BlockSpec((tm, d), lambda i: (i, gh0 // d)),
            pl.BlockSpec((tm, d), lambda i: (i, gg0 // d)),
            full((1, c)), full((1, dv)), full((c, d)), full((dvt, d)), full((d, d)),
        ],
        out_specs=pl.BlockSpec((tm, d), row),
        out_shape=jax.ShapeDtypeStruct((r, d), F32),
        compiler_params=_cparams("parallel"),
        name="mix_out",
    )(x2, mod_l, x0, z, rconv, o_f, o_b, p, p, p, skip, gn, w_bhy, w_bgla, w_o)


def _mlp_body(x_ref, mod_ref, g_ref, w1_ref, w2_ref, fg_ref, o_ref, h_scr, acc_scr, *, d, final_norm):
    f = pl.program_id(1)

    @pl.when(f == 0)
    def _():
        h_scr[...] = _modulated_norm(x_ref[...], g_ref[...], mod_ref[...], 3, 4, d).astype(BF16)
        acc_scr[...] = jnp.zeros_like(acc_scr)

    a = jnp.maximum(jnp.dot(h_scr[...], w1_ref[...], preferred_element_type=F32), 0.0)
    acc_scr[...] += jnp.dot((a * a).astype(BF16), w2_ref[...], preferred_element_type=F32)

    @pl.when(f == pl.num_programs(1) - 1)
    def _():
        g2 = mod_ref[:, 5 * d:6 * d]
        y = x_ref[...] + g2 * acc_scr[...]
        if final_norm:
            y = y * lax.rsqrt(jnp.mean(y * y, axis=-1, keepdims=True) + EPS) * fg_ref[...]
        o_ref[...] = y


def _mlp(x2, mod_l, norm_g, w1, w2, final_g, rows_per_cond, cond_base, final_norm):
    r, d = x2.shape
    dff = w1.shape[1]
    tm = _tile(min(r, rows_per_cond), 1024)
    tf = _tile(dff, 1024)
    bpc = rows_per_cond // tm
    return pl.pallas_call(
        functools.partial(_mlp_body, d=d, final_norm=final_norm),
        grid=(r // tm, dff // tf),
        in_specs=[
            pl.BlockSpec((tm, d), lambda i, f: (i, 0)),
            pl.BlockSpec((None, 1, N_MOD * d), lambda i, f: (cond_base + i // bpc, 0, 0)),
            pl.BlockSpec((1, d), lambda i, f: (0, 0)),
            pl.BlockSpec((d, tf), lambda i, f: (0, f)),
            pl.BlockSpec((tf, d), lambda i, f: (f, 0)),
            pl.BlockSpec((1, d), lambda i, f: (0, 0)),
        ],
        out_specs=pl.BlockSpec((tm, d), lambda i, f: (i, 0)),
        out_shape=jax.ShapeDtypeStruct((r, d), F32),
        scratch_shapes=[pltpu.VMEM((tm, d), BF16), pltpu.VMEM((tm, d), F32)],
        compiler_params=_cparams("parallel", "arbitrary"),
        name="mlp",
    )(x2, mod_l, norm_g, w1, w2, final_g)


def kernel(x, c, ctx, c_ctx, ada_w, ada_b, norm1_g, norm2_g, w_in, hy_conv_w, hy_conv_b, hy_filt_w1, hy_filt_b1, hy_filt_w2, hy_filt_b2, hy_filt_w3, hy_filt_freq, hy_decay, hy_skip, gla_gate_w, gla_gate_b, gla_norm_g, w_branch_hy, w_branch_gla, w_out, mlp_w1, mlp_w2, final_g):
    nb, seq, d = x.shape
    lctx = ctx.shape[1]
    depth = ada_w.shape[0]
    chy = hy_decay.shape[-1]
    rank = gla_gate_w.shape[2]
    dkt = gla_gate_w.shape[3]
    dvt = w_branch_gla.shape[1]
    dk, dv = dkt // GLA_HEADS, dvt // GLA_HEADS
    assert nb + 1 <= COND_ROWS and 2 * rank <= LANES

    cond = jnp.zeros((COND_ROWS, d), F32).at[:nb].set(c).at[nb].set(c_ctx)
    mod = _ada_mod(cond, ada_w, ada_b).reshape(depth, COND_ROWS, 1, N_MOD * d)

    sizes = (3 * chy, dkt, dkt, dvt, dvt, rank, rank, d, d)
    offs = np.concatenate([[0], np.cumsum(sizes)])
    seg = lambda w, i: w[..., offs[i]:offs[i + 1]]
    tc = _tile(chy, 512)
    w_hy = seg(w_in, 0).reshape(depth, d, 3, chy // tc, tc).transpose(0, 3, 1, 2, 4)
    w_hy = w_hy.reshape(depth, chy // tc, d, 3 * tc).astype(BF16)
    w_main = jnp.concatenate([seg(w_in, i) for i in (1, 2, 3, 4, 7, 8)], axis=-1).astype(BF16)
    w_lr = jnp.concatenate([seg(w_in, 5), seg(w_in, 6),
                            jnp.zeros((depth, d, LANES - 2 * rank), F32)], axis=-1).astype(BF16)
    q0 = 0
    k0 = q0 + dkt
    v0 = k0 + dkt
    rg0 = v0 + dvt
    gh0 = rg0 + dvt
    gg0 = gh0 + d

    gate_w_pad = jnp.zeros((depth, 2, LANES, dkt), F32)
    gate_w_pad = gate_w_pad.at[:, 0, :rank].set(gla_gate_w[:, 0]).at[:, 1, rank:2 * rank].set(gla_gate_w[:, 1])
    gate_w_pad = gate_w_pad.astype(BF16)

    w_bhy = w_branch_hy.astype(BF16)
    w_bgla = w_branch_gla.astype(BF16)
    w_o = w_out.astype(BF16)
    w1 = mlp_w1.astype(BF16)
    w2 = mlp_w2.astype(BF16)

    dft_lat, dft_ctx = _dft_constants(seq), _dft_constants(lctx)
    feat_lat, feat_ctx = _filter_features(seq), _filter_features(lctx)
    zero_state = jnp.zeros((nb, GLA_HEADS, dk, dv), F32)

    xs = x.reshape(nb * seq, d)
    cs = ctx.reshape(nb * lctx, d)
    for l in range(depth):
        last = l == depth - 1
        mod_l = mod[l]
        n1 = norm1_g[l].reshape(1, d)
        filt = (hy_filt_w1[l], hy_filt_b1[l], hy_filt_w2[l], hy_filt_b2[l], hy_filt_w3[l],
                hy_filt_freq[l], hy_decay[l])
        skip = hy_skip[l].reshape(1, chy)
        gn = gla_norm_g[l].reshape(1, dv)

        p_c, lr_c = _in_proj(cs, mod_l, n1, w_main[l], w_lr[l], nb * lctx, nb)
        of_c, ob_c, sf_c, sb_c = _gla_scan(p_c, lr_c, gate_w_pad[l], gla_gate_b[l], zero_state, zero_state,
                                           lctx, (q0, k0, v0))
        p_l, lr_l = _in_proj(xs, mod_l, n1, w_main[l], w_lr[l], seq, 0)
        of_l, ob_l, _, _ = _gla_scan(p_l, lr_l, gate_w_pad[l], gla_gate_b[l], sf_c, sb_c, seq, (q0, k0, v0))

        x0_l, z_l = _hyena_proj(xs, mod_l, n1, w_hy[l], hy_conv_w[l], hy_conv_b[l], seq, 0, GRID_W)
        h_l = _filter_spectrum(_hyena_filter_parts(feat_lat, *filt), dft_lat, seq)
        r_l = _long_conv(z_l, h_l, dft_lat, seq)
        xs = _mix_out(xs, mod_l, x0_l, z_l, r_l, of_l, ob_l, p_l, (rg0, gh0, gg0), skip, gn,
                      w_bhy[l], w_bgla[l], w_o[l], seq, 0)
        if not last:
            x0_c, z_c = _hyena_proj(cs, mod_l, n1, w_hy[l], hy_conv_w[l], hy_conv_b[l], nb * lctx, nb, lctx)
            h_c = _filter_spectrum(_hyena_filter_parts(feat_ctx, *filt), dft_ctx, lctx)
            r_c = _long_conv(z_c, h_c, dft_ctx, lctx)
            cs = _mix_out(cs, mod_l, x0_c, z_c, r_c, of_c, ob_c, p_c, (rg0, gh0, gg0), skip, gn,
                          w_bhy[l], w_bgla[l], w_o[l], nb * lctx, nb)

        n2 = norm2_g[l].reshape(1, d)
        fg = final_g.reshape(1, d)
        xs = _mlp(xs, mod_l, n2, w1[l], w2[l], fg, seq, 0, last)
        if not last:
            cs = _mlp(cs, mod_l, n2, w1[l], w2[l], fg, nb * lctx, nb, False)
    return xs.reshape(nb, seq, d)
```

```python
import functools
import math

import numpy as np
import jax
import jax.numpy as jnp
from jax import lax
from jax.experimental import pallas as pl
from jax.experimental.pallas import tpu as pltpu

F32 = jnp.float32
BF16 = jnp.bfloat16

GRID_W = 64
N_BANDS = 16
GLA_HEADS = 4
GLA_CHUNK = 64
GATE_TAU = 16.0
EPS = 1e-6
N_MOD = 6

V7X_VMEM_BYTES = 64 * 1024 * 1024
VMEM_LIMIT_BYTES = V7X_VMEM_BYTES * 3 // 4
LANES = 128
SUBLANES = 8
COND_ROWS = SUBLANES


def _cparams(*sem):
    return pltpu.CompilerParams(dimension_semantics=sem, vmem_limit_bytes=VMEM_LIMIT_BYTES)


def _tile(n, pref):
    t = min(n, pref)
    while n % t:
        t //= 2
    return t


def _sigmoid(x):
    return 1.0 / (1.0 + jnp.exp(-x))


def _silu(x):
    return x * _sigmoid(x)


def _log_sigmoid(x):
    return jnp.minimum(x, 0.0) - jnp.log(1.0 + jnp.exp(-jnp.abs(x)))


def _modulated_norm(x, g, mod, shift_idx, scale_idx, d):
    y = x * lax.rsqrt(jnp.mean(x * x, axis=-1, keepdims=True) + EPS) * g
    shift = mod[:, shift_idx * d:(shift_idx + 1) * d]
    scale = mod[:, scale_idx * d:(scale_idx + 1) * d]
    return y * (1.0 + scale) + shift


def _ada_body(c_ref, w_ref, b_ref, o_ref):
    s = _silu(c_ref[...]).astype(BF16)
    o_ref[...] = jnp.dot(s, w_ref[...].astype(BF16), preferred_element_type=F32) + b_ref[...]


def _ada_mod(cond, ada_w, ada_b):
    depth, d, n = ada_w.shape
    tn = _tile(n, 1536)
    return pl.pallas_call(
        _ada_body,
        grid=(depth, n // tn),
        in_specs=[
            pl.BlockSpec((COND_ROWS, d), lambda l, j: (0, 0)),
            pl.BlockSpec((None, d, tn), lambda l, j: (l, 0, j)),
            pl.BlockSpec((None, 1, tn), lambda l, j: (l, 0, j)),
        ],
        out_specs=pl.BlockSpec((None, COND_ROWS, tn), lambda l, j: (l, 0, j)),
        out_shape=jax.ShapeDtypeStruct((depth, COND_ROWS, n), F32),
        compiler_params=_cparams("parallel", "parallel"),
        name="ada_mod",
    )(cond, ada_w, ada_b.reshape(depth, 1, n))


def _in_proj_body(x_ref, mod_ref, g_ref, w_ref, wlr_ref, o_ref, olr_ref, h_scr, *, d):
    @pl.when(pl.program_id(1) == 0)
    def _():
        h = _modulated_norm(x_ref[...], g_ref[...], mod_ref[...], 0, 1, d).astype(BF16)
        h_scr[...] = h
        olr_ref[...] = jnp.dot(h, wlr_ref[...], preferred_element_type=F32)

    o_ref[...] = jnp.dot(h_scr[...], w_ref[...], preferred_element_type=F32).astype(o_ref.dtype)


def _in_proj(x2, mod_l, norm_g, w_main, w_lr, rows_per_cond, cond_base):
    r, d = x2.shape
    n = w_main.shape[1]
    tm = _tile(min(r, rows_per_cond), 1024)
    tn = n // 2 if n % (2 * LANES) == 0 else n
    blocks_per_cond = rows_per_cond // tm
    return pl.pallas_call(
        functools.partial(_in_proj_body, d=d),
        grid=(r // tm, n // tn),
        in_specs=[
            pl.BlockSpec((tm, d), lambda i, j: (i, 0)),
            pl.BlockSpec((None, 1, N_MOD * d), lambda i, j: (cond_base + i // blocks_per_cond, 0, 0)),
            pl.BlockSpec((1, d), lambda i, j: (0, 0)),
            pl.BlockSpec((d, tn), lambda i, j: (0, j)),
            pl.BlockSpec((d, LANES), lambda i, j: (0, 0)),
        ],
        out_specs=[
            pl.BlockSpec((tm, tn), lambda i, j: (i, j)),
            pl.BlockSpec((tm, LANES), lambda i, j: (i, 0)),
        ],
        out_shape=[jax.ShapeDtypeStruct((r, n), BF16), jax.ShapeDtypeStruct((r, LANES), F32)],
        scratch_shapes=[pltpu.VMEM((tm, d), BF16)],
        compiler_params=_cparams("parallel", "arbitrary"),
        name="in_proj",
    )(x2, mod_l, norm_g, w_main, w_lr)


def _hyena_proj_body(x_ref, mod_ref, g_ref, w0_ref, w1_ref, w2_ref, cw0_ref, cw1_ref, cw2_ref,
                     cb0_ref, cb1_ref, cb2_ref, x0_ref, z_ref, h_scr, *, d, period):
    @pl.when(pl.program_id(1) == 0)
    def _():
        h_scr[...] = _modulated_norm(x_ref[...], g_ref[...], mod_ref[...], 0, 1, d).astype(BF16)

    tm = x_ref.shape[0]
    pos = lax.broadcasted_iota(jnp.int32, (tm, 1), 0) % period
    first, last = pos == 0, pos == period - 1

    def conv(w_ref, cw_ref, cb_ref):
        p = jnp.dot(h_scr[...], w_ref[...], preferred_element_type=F32)
        prev = jnp.where(first, 0.0, pltpu.roll(p, 1, axis=0))
        nxt = jnp.where(last, 0.0, pltpu.roll(p, tm - 1, axis=0))
        return prev * cw_ref[0:1, :] + p * cw_ref[1:2, :] + nxt * cw_ref[2:3, :] + cb_ref[...]

    x0_ref[...] = conv(w0_ref, cw0_ref, cb0_ref).astype(BF16)
    z_ref[...] = (conv(w1_ref, cw1_ref, cb1_ref) * conv(w2_ref, cw2_ref, cb2_ref)).astype(BF16)


def _hyena_proj(x2, mod_l, norm_g, w_hy, conv_w, conv_b, rows_per_cond, cond_base, period):
    r, d = x2.shape
    c = w_hy.shape[1] // 3
    tc = _tile(c, 512)
    ncj = c // tc
    tm = _tile(min(r, rows_per_cond), 1024)
    assert tm % period == 0
    blocks_per_cond = rows_per_cond // tm
    part = lambda rows, p: pl.BlockSpec((rows, tc), lambda i, j: (0, p * ncj + j))
    return pl.pallas_call(
        functools.partial(_hyena_proj_body, d=d, period=period),
        grid=(r // tm, ncj),
        in_specs=[
            pl.BlockSpec((tm, d), lambda i, j: (i, 0)),
            pl.BlockSpec((None, 1, N_MOD * d), lambda i, j: (cond_base + i // blocks_per_cond, 0, 0)),
            pl.BlockSpec((1, d), lambda i, j: (0, 0)),
            part(d, 0), part(d, 1), part(d, 2), part(3, 0), part(3, 1), part(3, 2), part(1, 0), part(1, 1), part(1, 2),
        ],
        out_specs=[pl.BlockSpec((tm, tc), lambda i, j: (i, j)), pl.BlockSpec((tm, tc), lambda i, j: (i, j))],
        out_shape=[jax.ShapeDtypeStruct((r, c), BF16), jax.ShapeDtypeStruct((r, c), BF16)],
        scratch_shapes=[pltpu.VMEM((tm, d), BF16)],
        compiler_params=_cparams("parallel", "arbitrary"),
        name="hyena_proj",
    )(x2, mod_l, norm_g, w_hy, w_hy, w_hy, conv_w, conv_w, conv_w, conv_b, conv_b, conv_b)


def _filter_body(feat_ref, w1_ref, b1_ref, w2_ref, b2_ref, w3_ref, fr_ref, dec_ref, fb_ref, *, c, tl):
    i = pl.program_id(0)
    hp = lax.Precision.HIGHEST
    feat = feat_ref[...]
    fr = fr_ref[...]
    a = jnp.sin(fr * (jnp.dot(feat, w1_ref[...], precision=hp, preferred_element_type=F32) + b1_ref[...]))
    a = jnp.sin(fr * (jnp.dot(a, w2_ref[...], precision=hp, preferred_element_type=F32) + b2_ref[...]))
    hk = jnp.dot(a, w3_ref[...], precision=hp, preferred_element_type=F32)
    t = feat[:, 0:1]
    window = jnp.exp(-t * jnp.abs(dec_ref[...]))
    h_f = hk[:, :c] * window
    h_b = hk[:, c:] * window
    row = i * tl + lax.broadcasted_iota(jnp.int32, (tl, 1), 0)
    h_b = jnp.where(row == 0, 0.0, h_b)
    fb_ref[:, :c] = h_f.astype(BF16)
    fb_ref[:, c:] = h_b.astype(BF16)


def _pad_to(a, shape):
    return jnp.pad(a, [(0, s - n) for n, s in zip(a.shape, shape)])


def _hyena_filter_parts(feat, w1, b1, w2, b2, w3, freq, decay):
    l = feat.shape[0]
    c = decay.shape[-1]
    fe = fo = LANES
    assert feat.shape[1] <= fe and w1.shape[1] <= fo
    feat = _pad_to(feat, (l, fe))
    w1, w2, w3 = _pad_to(w1, (fe, fo)), _pad_to(w2, (fo, fo)), _pad_to(w3, (fo, 2 * c))
    b1, b2, freq = _pad_to(b1, (fo,)), _pad_to(b2, (fo,)), _pad_to(freq, (fo,))
    tl = _tile(l, 512)
    full = lambda shape: pl.BlockSpec(shape, lambda i: (0, 0))
    return pl.pallas_call(
        functools.partial(_filter_body, c=c, tl=tl),
        grid=(l // tl,),
        in_specs=[
            pl.BlockSpec((tl, fe), lambda i: (i, 0)),
            full((fe, fo)), full((1, fo)), full((fo, fo)), full((1, fo)), full((fo, 2 * c)),
            full((1, fo)), full((1, c)),
        ],
        out_specs=pl.BlockSpec((tl, 2 * c), lambda i: (i, 0)),
        out_shape=jax.ShapeDtypeStruct((l, 2 * c), BF16),
        compiler_params=_cparams("parallel"),
        name="hyena_filter",
    )(feat, w1, b1.reshape(1, fo), w2, b2.reshape(1, fo), w3, freq.reshape(1, fo), decay.reshape(1, c))


def _filter_features(l):
    t = jnp.linspace(0.0, 1.0, l, dtype=F32)[:, None]
    ang = (2.0 * math.pi / l) * jnp.arange(l, dtype=F32)[:, None] * \
        jnp.linspace(1e-4, N_BANDS - 1, N_BANDS, dtype=F32)[None, :]
    return jnp.concatenate([t, jnp.cos(ang), -jnp.sin(ang)], axis=-1)


DFT_BLOCK = 256
BF16_SUBLANES = 16


def _dft_plan(l):
    nb = min(DFT_BLOCK, l)
    nslab = l // nb
    grp = max(BF16_SUBLANES, LANES // nslab)
    assert l % nb == 0 and nb % grp == 0
    return nb, nslab, grp


def _dft_constants(l):
    nb, nslab, grp = _dft_plan(l)
    n = 2 * l
    nk1 = nslab + 1
    ngrp = nb // grp
    gi = np.arange(ngrp)[:, None, None]
    row = np.arange(nk1 * 2 * grp)[None, :, None]
    col = np.arange(nslab * grp)[None, None, :]
    k1, part, a_out = row // (2 * grp), (row // grp) % 2, row % grp
    n1, a_in = col // grp, col % grp
    t = nb * n1 + grp * gi + a_in
    ang = ((t * k1) % n) * (2.0 * math.pi / n)
    f1 = np.where(a_out == a_in, np.where(part == 0, np.cos(ang), -np.sin(ang)), 0.0)
    weight = np.where((k1 == 0) | (k1 == nslab), 1.0 / n, 2.0 / n)
    f1i = np.swapaxes(f1 * weight, 1, 2)
    r2 = np.arange(nb)
    th = ((r2[:, None] * r2[None, :]) % nb) * (2.0 * math.pi / nb)
    cs, sn = np.cos(th), np.sin(th)
    f2 = np.block([[cs, sn], [-sn, cs]])
    return tuple(jnp.asarray(m, F32).astype(BF16) for m in (f1, f1i, f2, f2.T))


def _unrolled_loop(n, width, fn):
    def body(i, carry):
        for j in range(width):
            fn(i * width + j)
        return carry

    if n >= width:
        lax.fori_loop(0, n // width, body, 0)
    for i in range(n - n % width, n):
        fn(i)


def _residue_pairs(nk1, fn):
    _unrolled_loop(nk1 // 2, 2, lambda i: fn((2 * i, 2 * i + 1)))
    if nk1 % 2:
        fn((nk1 - 1,))


def _residue_columns(t_scr, ks, nb):
    ct = t_scr.shape[-1]
    return jnp.concatenate([t_scr[k1].reshape(2 * nb, ct) for k1 in ks], axis=1)


def _dft_stage1(src_ref, f1_ref, t_scr, nb, nslab, grp):
    ct = src_ref.shape[-1]
    nk1 = nslab + 1

    def group(gi):
        r0 = pl.multiple_of(gi * grp, grp)
        data = src_ref[:, pl.ds(r0, grp), :].reshape(nslab * grp, ct)
        t = jnp.dot(f1_ref[gi], data, preferred_element_type=F32)
        t_scr[:, :, pl.ds(r0, grp), :] = t.astype(BF16).reshape(nk1, 2, grp, ct)

    _unrolled_loop(nb // grp, 4, group)


def _spectrum_body(fb_ref, f1_ref, f2_ref, h_ref, t_scr, *, nb, nslab, grp):
    s = pl.program_id(1)
    ct = fb_ref.shape[-1]
    _dft_stage1(fb_ref, f1_ref, t_scr, nb, nslab, grp)

    @pl.when(s == 0)
    def _():
        h_ref[...] = jnp.zeros_like(h_ref)

    imag_sign = jnp.where(s == 0, 1.0, -1.0)

    def residues(ks):
        x = jnp.dot(f2_ref[...], _residue_columns(t_scr, ks, nb), preferred_element_type=F32)
        for i, k1 in enumerate(ks):
            h_ref[k1, :nb, :] += x[:nb, i * ct:(i + 1) * ct]
            h_ref[k1, nb:, :] += imag_sign * x[nb:, i * ct:(i + 1) * ct]

    _residue_pairs(nslab + 1, residues)


def _filter_spectrum(fb, consts, l):
    nb, nslab, grp = _dft_plan(l)
    f1, _, f2, _ = consts
    c = fb.shape[1] // 2
    ct = _tile(c, 256)
    ncj = c // ct
    nk1 = nslab + 1
    const = lambda a: pl.BlockSpec(a.shape, lambda j, s: (0,) * a.ndim)
    return pl.pallas_call(
        functools.partial(_spectrum_body, nb=nb, nslab=nslab, grp=grp),
        grid=(ncj, 2),
        in_specs=[pl.BlockSpec((nslab, nb, ct), lambda j, s: (0, 0, s * ncj + j)), const(f1), const(f2)],
        out_specs=pl.BlockSpec((nk1, 2 * nb, ct), lambda j, s: (0, 0, j)),
        out_shape=jax.ShapeDtypeStruct((nk1, 2 * nb, c), F32),
        scratch_shapes=[pltpu.VMEM((nk1, 2, nb, ct), BF16)],
        compiler_params=_cparams("parallel", "arbitrary"),
        name="filter_spectrum",
    )(fb.reshape(nslab, nb, 2 * c), f1, f2)


def _long_conv_body(z_ref, h_ref, f1_ref, f1i_ref, f2_ref, f2i_ref, r_ref, t_scr, *, nb, nslab, grp):
    ct = z_ref.shape[-1]
    nk1 = nslab + 1
    _dft_stage1(z_ref, f1_ref, t_scr, nb, nslab, grp)

    def spectral(ks):
        x = jnp.dot(f2_ref[...], _residue_columns(t_scr, ks, nb), preferred_element_type=F32)
        xr, xi = x[:nb], x[nb:]
        hr = jnp.concatenate([h_ref[k1, :nb, :] for k1 in ks], axis=1)
        hi = jnp.concatenate([h_ref[k1, nb:, :] for k1 in ks], axis=1)
        y = jnp.concatenate([xr * hr - xi * hi, xr * hi + xi * hr], axis=0).astype(BF16)
        u = jnp.dot(f2i_ref[...], y, preferred_element_type=F32).astype(BF16)
        for i, k1 in enumerate(ks):
            t_scr[k1] = u[:, i * ct:(i + 1) * ct].reshape(2, nb, ct)

    _residue_pairs(nk1, spectral)

    def inverse1(gi):
        r0 = pl.multiple_of(gi * grp, grp)
        data = t_scr[:, :, pl.ds(r0, grp), :].reshape(nk1 * 2 * grp, ct)
        out = jnp.dot(f1i_ref[gi], data, preferred_element_type=F32)
        r_ref[:, pl.ds(r0, grp), :] = out.astype(r_ref.dtype).reshape(nslab, grp, ct)

    _unrolled_loop(nb // grp, 4, inverse1)


def _long_conv(zb, hspec, consts, l):
    nb, nslab, grp = _dft_plan(l)
    f1, f1i, f2, f2i = consts
    rows, c = zb.shape
    nbatch = rows // l
    ct = _tile(c, 256)
    nk1 = nslab + 1
    const = lambda a: pl.BlockSpec(a.shape, lambda j, b: (0,) * a.ndim)
    seq_spec = pl.BlockSpec((None, nslab, nb, ct), lambda j, b: (b, 0, 0, j))
    out = pl.pallas_call(
        functools.partial(_long_conv_body, nb=nb, nslab=nslab, grp=grp),
        grid=(c // ct, nbatch),
        in_specs=[seq_spec, pl.BlockSpec((nk1, 2 * nb, ct), lambda j, b: (0, 0, j)),
                  const(f1), const(f1i), const(f2), const(f2i)],
        out_specs=seq_spec,
        out_shape=jax.ShapeDtypeStruct((nbatch, nslab, nb, c), BF16),
        scratch_shapes=[pltpu.VMEM((nk1, 2, nb, ct), BF16)],
        compiler_params=_cparams("parallel", "parallel"),
        name="long_conv",
    )(zb.reshape(nbatch, nslab, nb, c), hspec, f1, f1i, f2, f2i)
    return out.reshape(rows, c)


def _split_bf16(x):
    hi = x.astype(BF16)
    lo = (x - hi.astype(F32)).astype(BF16)
    return hi, lo


def _gla_body(qf_ref, kf_ref, vf_ref, lrf_ref, qb_ref, kb_ref, vb_ref, lrb_ref, gw_ref, gb_ref,
              s0f_ref, s0b_ref, of_ref, ob_ref, sf_ref, sb_ref, *, dk, dv, gsz):
    step = pl.program_id(1)
    ck = GLA_CHUNK
    tb = gsz * ck

    @pl.when(step == 0)
    def _():
        sf_ref[...] = s0f_ref[...]
        sb_ref[...] = s0b_ref[...]

    rows = lax.broadcasted_iota(jnp.int32, (tb, tb), 0)
    cols = lax.broadcasted_iota(jnp.int32, (tb, tb), 1)
    same_chunk = (rows // ck) == (cols // ck)
    scale = dk ** -0.5

    def direction(q_ref, k_ref, v_ref, lr_ref, o_ref, s_ref, d):
        fwd = d == 0
        g = _log_sigmoid(jnp.dot(lr_ref[...].astype(BF16), gw_ref[d], preferred_element_type=F32)
                         + gb_ref[d]) / GATE_TAU
        keep = same_chunk & ((cols <= rows) if fwd else (cols >= rows))
        tri = keep.astype(BF16)
        g_hi, g_lo = _split_bf16(g)
        bcum = jnp.dot(tri, g_hi, preferred_element_type=F32) + jnp.dot(tri, g_lo, preferred_element_type=F32)
        ends = [bcum[(c * ck + ck - 1 if fwd else c * ck):(c * ck + ck if fwd else c * ck + 1), :] for c in range(gsz)]
        b_end = jnp.concatenate([jnp.broadcast_to(e, (ck, e.shape[1])) for e in ends], axis=0)
        pad = [jnp.zeros_like(ends[0])] * (SUBLANES - gsz)
        decay_t = jnp.exp(jnp.concatenate(ends + pad, axis=0)).T
        q = q_ref[...].astype(F32) * scale
        k = k_ref[...].astype(F32)
        q_t = (q * jnp.exp(bcum)).astype(BF16)
        k_t = (k * jnp.exp(-bcum)).astype(BF16)
        k_d = (k * jnp.exp(b_end - bcum)).astype(BF16)
        order = range(gsz) if fwd else range(gsz - 1, -1, -1)
        for h in range(GLA_HEADS):
            kc = slice(h * dk, (h + 1) * dk)
            vc = slice(h * dv, (h + 1) * dv)
            v = v_ref[:, vc]
            att = lax.dot_general(q_t[:, kc], k_t[:, kc], (((1,), (1,)), ((), ())), preferred_element_type=F32)
            o_intra = jnp.dot(jnp.where(keep, att, 0.0).astype(BF16), v, preferred_element_type=F32)
            s = s_ref[h]
            for c in order:
                rs = slice(c * ck, (c + 1) * ck)
                o = o_intra[rs] + jnp.dot(q_t[rs, kc], s.astype(BF16), preferred_element_type=F32)
                o_ref[rs, vc] = o.astype(o_ref.dtype)
                upd = lax.dot_general(k_d[rs, kc], v[rs], (((0,), (0,)), ((), ())), preferred_element_type=F32)
                s = decay_t[kc, c:c + 1] * s + upd
            s_ref[h] = s

    direction(qf_ref, kf_ref, vf_ref, lrf_ref, of_ref, sf_ref, 0)
    direction(qb_ref, kb_ref, vb_ref, lrb_ref, ob_ref, sb_ref, 1)


def _gla_scan(p, lr, gate_w, gate_b, s0_f, s0_b, seq_len, cols):
    r = p.shape[0]
    nb = r // seq_len
    dkt = gate_w.shape[2]
    dk = dkt // GLA_HEADS
    dv = s0_f.shape[-1]
    dvt = dv * GLA_HEADS
    gsz = _tile(seq_len // GLA_CHUNK, 4)
    assert gsz <= SUBLANES
    tb = gsz * GLA_CHUNK
    n = seq_len // tb
    q0, k0, v0 = cols
    fwd_row = lambda b, i: b * n + i
    bwd_row = lambda b, i: b * n + (n - 1 - i)

    def specs(row):
        return [
            pl.BlockSpec((tb, dkt), lambda b, i: (row(b, i), q0 // dkt)),
            pl.BlockSpec((tb, dkt), lambda b, i: (row(b, i), k0 // dkt)),
            pl.BlockSpec((tb, dvt), lambda b, i: (row(b, i), v0 // dvt)),
            pl.BlockSpec((tb, LANES), lambda b, i: (row(b, i), 0)),
        ]

    state_spec = pl.BlockSpec((None, GLA_HEADS, dk, dv), lambda b, i: (b, 0, 0, 0))
    return pl.pallas_call(
        functools.partial(_gla_body, dk=dk, dv=dv, gsz=gsz),
        grid=(nb, n),
        in_specs=specs(fwd_row) + specs(bwd_row) + [
            pl.BlockSpec((2, LANES, dkt), lambda b, i: (0, 0, 0)),
            pl.BlockSpec((2, 1, dkt), lambda b, i: (0, 0, 0)),
            state_spec, state_spec,
        ],
        out_specs=[
            pl.BlockSpec((tb, dvt), lambda b, i: (fwd_row(b, i), 0)),
            pl.BlockSpec((tb, dvt), lambda b, i: (bwd_row(b, i), 0)),
            state_spec, state_spec,
        ],
        out_shape=[jax.ShapeDtypeStruct((r, dvt), BF16), jax.ShapeDtypeStruct((r, dvt), BF16),
                   jax.ShapeDtypeStruct(s0_f.shape, F32), jax.ShapeDtypeStruct(s0_b.shape, F32)],
        compiler_params=_cparams("parallel", "arbitrary"),
        name="gla_scan",
    )(p, p, p, lr, p, p, p, lr, gate_w, gate_b.reshape(2, 1, dkt), s0_f, s0_b)


def _mix_out_body(x_ref, mod_ref, x0_ref, z_ref, r_ref, of_ref, ob_ref, rg_ref, gh_ref, gg_ref,
                  skip_ref, gn_ref, wh_ref, wg_ref, wo_ref, o_ref, *, d, dv):
    f32 = lambda ref: ref[...].astype(F32)
    y_hy = f32(x0_ref) * (f32(r_ref) + skip_ref[...] * f32(z_ref))
    o = f32(of_ref) + f32(ob_ref)
    rg = f32(rg_ref)
    parts = []
    for h in range(GLA_HEADS):
        oh = o[:, h * dv:(h + 1) * dv]
        on = oh * lax.rsqrt(jnp.mean(oh * oh, axis=-1, keepdims=True) + EPS) * gn_ref[...]
        parts.append((on * _silu(rg[:, h * dv:(h + 1) * dv])).astype(BF16))
    y_gla = jnp.concatenate(parts, axis=-1)
    ph = jnp.dot(y_hy.astype(BF16), wh_ref[...], preferred_element_type=F32)
    pg = jnp.dot(y_gla, wg_ref[...], preferred_element_type=F32)
    merged = _sigmoid(f32(gh_ref)) * ph + _sigmoid(f32(gg_ref)) * pg
    out = jnp.dot(merged.astype(BF16), wo_ref[...], preferred_element_type=F32)
    g1 = mod_ref[:, 2 * d:3 * d]
    o_ref[...] = x_ref[...] + g1 * out


def _mix_out(x2, mod_l, x0, z, rconv, o_f, o_b, p, cols, skip, gn, w_bhy, w_bgla, w_o,
             rows_per_cond, cond_base):
    r, d = x2.shape
    c = x0.shape[1]
    dvt = o_f.shape[1]
    dv = dvt // GLA_HEADS
    rg0, gh0, gg0 = cols
    tm = _tile(min(r, rows_per_cond), 512)
    bpc = rows_per_cond // tm
    row = lambda i: (i, 0)
    full = lambda shape: pl.BlockSpec(shape, lambda i: (0, 0))
    return pl.pallas_call(
        functools.partial(_mix_out_body, d=d, dv=dv),
        grid=(r // tm,),
        in_specs=[
            pl.BlockSpec((tm, d), row),
            pl.BlockSpec((None, 1, N_MOD * d), lambda i: (cond_base + i // bpc, 0, 0)),
            pl.BlockSpec((tm, c), row),
            pl.BlockSpec((tm, c), row),
            pl.BlockSpec((tm, c), row),
            pl.BlockSpec((tm, dvt), row),
            pl.BlockSpec((tm, dvt), row),
            pl.BlockSpec((tm, dvt), lambda i: (i, rg0 // dvt)),
            pl.BlockSpec((tm, d), lambda i: (i, gh0 // d)),
            pl.BlockSpec((tm, d), lambda i: (i, gg0 // d)),
            full((1, c)), full((1, dv)), full((c, d)), full((dvt, d)), full((d, d)),
        ],
        out_specs=pl.BlockSpec((tm, d), row),
        out_shape=jax.ShapeDtypeStruct((r, d), F32),
        compiler_params=_cparams("parallel"),
        name="mix_out",
    )(x2, mod_l, x0, z, rconv, o_f, o_b, p, p, p, skip, gn, w_bhy, w_bgla, w_o)


def _mlp_body(x_ref, mod_ref, g_ref, w1_ref, w2_ref, fg_ref, o_ref, h_scr, acc_scr, *, d, final_norm):
    f = pl.program_id(1)

    @pl.when(f == 0)
    def _():
        h_scr[...] = _modulated_norm(x_ref[...], g_ref[...], mod_ref[...], 3, 4, d).astype(BF16)
        acc_scr[...] = jnp.zeros_like(acc_scr)

    a = jnp.maximum(jnp.dot(h_scr[...], w1_ref[...], preferred_element_type=F32), 0.0)
    acc_scr[...] += jnp.dot((a * a).astype(BF16), w2_ref[...], preferred_element_type=F32)

    @pl.when(f == pl.num_programs(1) - 1)
    def _():
        g2 = mod_ref[:, 5 * d:6 * d]
        y = x_ref[...] + g2 * acc_scr[...]
        if final_norm:
            y = y * lax.rsqrt(jnp.mean(y * y, axis=-1, keepdims=True) + EPS) * fg_ref[...]
        o_ref[...] = y


def _mlp(x2, mod_l, norm_g, w1, w2, final_g, rows_per_cond, cond_base, final_norm):
    r, d = x2.shape
    dff = w1.shape[1]
    tm = _tile(min(r, rows_per_cond), 1024)
    tf = _tile(dff, 1024)
    bpc = rows_per_cond // tm
    return pl.pallas_call(
        functools.partial(_mlp_body, d=d, final_norm=final_norm),
        grid=(r // tm, dff // tf),
        in_specs=[
            pl.BlockSpec((tm, d), lambda i, f: (i, 0)),
            pl.BlockSpec((None, 1, N_MOD * d), lambda i, f: (cond_base + i // bpc, 0, 0)),
            pl.BlockSpec((1, d), lambda i, f: (0, 0)),
            pl.BlockSpec((d, tf), lambda i, f: (0, f)),
            pl.BlockSpec((tf, d), lambda i, f: (f, 0)),
            pl.BlockSpec((1, d), lambda i, f: (0, 0)),
        ],
        out_specs=pl.BlockSpec((tm, d), lambda i, f: (i, 0)),
        out_shape=jax.ShapeDtypeStruct((r, d), F32),
        scratch_shapes=[pltpu.VMEM((tm, d), BF16), pltpu.VMEM((tm, d), F32)],
        compiler_params=_cparams("parallel", "arbitrary"),
        name="mlp",
    )(x2, mod_l, norm_g, w1, w2, final_g)


def kernel(x, c, ctx, c_ctx, ada_w, ada_b, norm1_g, norm2_g, w_in, hy_conv_w, hy_conv_b, hy_filt_w1, hy_filt_b1, hy_filt_w2, hy_filt_b2, hy_filt_w3, hy_filt_freq, hy_decay, hy_skip, gla_gate_w, gla_gate_b, gla_norm_g, w_branch_hy, w_branch_gla, w_out, mlp_w1, mlp_w2, final_g):
    nb, seq, d = x.shape
    lctx = ctx.shape[1]
    depth = ada_w.shape[0]
    chy = hy_decay.shape[-1]
    rank = gla_gate_w.shape[2]
    dkt = gla_gate_w.shape[3]
    dvt = w_branch_gla.shape[1]
    dk, dv = dkt // GLA_HEADS, dvt // GLA_HEADS
    assert nb + 1 <= COND_ROWS and 2 * rank <= LANES

    cond = jnp.zeros((COND_ROWS, d), F32).at[:nb].set(c).at[nb].set(c_ctx)
    mod = _ada_mod(cond, ada_w, ada_b).reshape(depth, COND_ROWS, 1, N_MOD * d)

    sizes = (3 * chy, dkt, dkt, dvt, dvt, rank, rank, d, d)
    offs = np.concatenate([[0], np.cumsum(sizes)])
    w_hy = w_in[..., :offs[1]].astype(BF16)
    w_main = jnp.concatenate([w_in[..., offs[1]:offs[5]], w_in[..., offs[7]:]], axis=-1).astype(BF16)
    w_lr = jnp.pad(w_in[..., offs[5]:offs[7]], ((0, 0), (0, 0), (0, LANES - 2 * rank))).astype(BF16)
    q0 = 0
    k0 = q0 + dkt
    v0 = k0 + dkt
    rg0 = v0 + dvt
    gh0 = rg0 + dvt
    gg0 = gh0 + d

    gate_w_pad = jnp.zeros((depth, 2, LANES, dkt), F32)
    gate_w_pad = gate_w_pad.at[:, 0, :rank].set(gla_gate_w[:, 0]).at[:, 1, rank:2 * rank].set(gla_gate_w[:, 1])
    gate_w_pad = gate_w_pad.astype(BF16)

    w_bhy = w_branch_hy.astype(BF16)
    w_bgla = w_branch_gla.astype(BF16)
    w_o = w_out.astype(BF16)
    w1 = mlp_w1.astype(BF16)
    w2 = mlp_w2.astype(BF16)

    dft_lat, dft_ctx = _dft_constants(seq), _dft_constants(lctx)
    feat_lat, feat_ctx = _filter_features(seq), _filter_features(lctx)
    zero_state = jnp.zeros((nb, GLA_HEADS, dk, dv), F32)

    xs = x.reshape(nb * seq, d)
    cs = ctx.reshape(nb * lctx, d)
    for l in range(depth):
        last = l == depth - 1
        mod_l = mod[l]
        n1 = norm1_g[l].reshape(1, d)
        filt = (hy_filt_w1[l], hy_filt_b1[l], hy_filt_w2[l], hy_filt_b2[l], hy_filt_w3[l],
                hy_filt_freq[l], hy_decay[l])
        skip = hy_skip[l].reshape(1, chy)
        gn = gla_norm_g[l].reshape(1, dv)

        p_c, lr_c = _in_proj(cs, mod_l, n1, w_main[l], w_lr[l], nb * lctx, nb)
        of_c, ob_c, sf_c, sb_c = _gla_scan(p_c, lr_c, gate_w_pad[l], gla_gate_b[l], zero_state, zero_state,
                                           lctx, (q0, k0, v0))
        p_l, lr_l = _in_proj(xs, mod_l, n1, w_main[l], w_lr[l], seq, 0)
        of_l, ob_l, _, _ = _gla_scan(p_l, lr_l, gate_w_pad[l], gla_gate_b[l], sf_c, sb_c, seq, (q0, k0, v0))

        conv_b = hy_conv_b[l].reshape(1, 3 * chy)
        x0_l, z_l = _hyena_proj(xs, mod_l, n1, w_hy[l], hy_conv_w[l], conv_b, seq, 0, GRID_W)
        h_l = _filter_spectrum(_hyena_filter_parts(feat_lat, *filt), dft_lat, seq)
        r_l = _long_conv(z_l, h_l, dft_lat, seq)
        xs = _mix_out(xs, mod_l, x0_l, z_l, r_l, of_l, ob_l, p_l, (rg0, gh0, gg0), skip, gn,
                      w_bhy[l], w_bgla[l], w_o[l], seq, 0)
        if not last:
            x0_c, z_c = _hyena_proj(cs, mod_l, n1, w_hy[l], hy_conv_w[l], conv_b, nb * lctx, nb, lctx)
            h_c = _filter_spectrum(_hyena_filter_parts(feat_ctx, *filt), dft_ctx, lctx)
            r_c = _long_conv(z_c, h_c, dft_ctx, lctx)
            cs = _mix_out(cs, mod_l, x0_c, z_c, r_c, of_c, ob_c, p_c, (rg0, gh0, gg0), skip, gn,
                          w_bhy[l], w_bgla[l], w_o[l], nb * lctx, nb)

        n2 = norm2_g[l].reshape(1, d)
        fg = final_g.reshape(1, d)
        xs = _mlp(xs, mod_l, n2, w1[l], w2[l], fg, seq, 0, last)
        if not last:
            cs = _mlp(cs, mod_l, n2, w1[l], w2[l], fg, nb * lctx, nb, False)
    return xs.reshape(nb, seq, d)
```

```python
import functools
import math

import numpy as np
import jax
import jax.numpy as jnp
from jax import lax
from jax.experimental import pallas as pl
from jax.experimental.pallas import tpu as pltpu

F32 = jnp.float32
BF16 = jnp.bfloat16

GRID_W = 64
N_BANDS = 16
GLA_HEADS = 4
GLA_CHUNK = 64
GATE_TAU = 16.0
EPS = 1e-6
N_MOD = 6

V7X_VMEM_BYTES = 64 * 1024 * 1024
VMEM_LIMIT_BYTES = V7X_VMEM_BYTES * 3 // 4
LANES = 128
SUBLANES = 8
COND_ROWS = SUBLANES


def _cparams(*sem):
    return pltpu.CompilerParams(dimension_semantics=sem, vmem_limit_bytes=VMEM_LIMIT_BYTES)


def _tile(n, pref):
    t = min(n, pref)
    while n % t:
        t //= 2
    return t


def _sigmoid(x):
    return 1.0 / (1.0 + jnp.exp(-x))


def _silu(x):
    return x * _sigmoid(x)


def _log_sigmoid(x):
    return jnp.minimum(x, 0.0) - jnp.log(1.0 + jnp.exp(-jnp.abs(x)))


def _modulated_norm(x, g, mod, shift_idx, scale_idx, d):
    y = x * lax.rsqrt(jnp.mean(x * x, axis=-1, keepdims=True) + EPS) * g
    shift = mod[:, shift_idx * d:(shift_idx + 1) * d]
    scale = mod[:, scale_idx * d:(scale_idx + 1) * d]
    return y * (1.0 + scale) + shift


def _ada_body(c_ref, w_ref, b_ref, o_ref):
    s = _silu(c_ref[...]).astype(BF16)
    o_ref[...] = jnp.dot(s, w_ref[...].astype(BF16), preferred_element_type=F32) + b_ref[...]


def _ada_mod(cond, ada_w, ada_b):
    depth, d, n = ada_w.shape
    tn = _tile(n, 1536)
    return pl.pallas_call(
        _ada_body,
        grid=(depth, n // tn),
        in_specs=[
            pl.BlockSpec((COND_ROWS, d), lambda l, j: (0, 0)),
            pl.BlockSpec((None, d, tn), lambda l, j: (l, 0, j)),
            pl.BlockSpec((None, 1, tn), lambda l, j: (l, 0, j)),
        ],
        out_specs=pl.BlockSpec((None, COND_ROWS, tn), lambda l, j: (l, 0, j)),
        out_shape=jax.ShapeDtypeStruct((depth, COND_ROWS, n), F32),
        compiler_params=_cparams("parallel", "parallel"),
        name="ada_mod",
    )(cond, ada_w, ada_b.reshape(depth, 1, n))


MAIN_CHUNK = 1024


def _in_proj_body(x_ref, mod_ref, g_ref, why_ref, cw_ref, cb_ref, wm_ref, wlr_ref,
                  x0_ref, z_ref, p_ref, lr_ref, h_scr, *, d, c, period):
    h_scr[...] = _modulated_norm(x_ref[...], g_ref[...], mod_ref[...], 0, 1, d).astype(BF16)
    tm = x_ref.shape[0]
    pos = lax.broadcasted_iota(jnp.int32, (tm, 1), 0) % period
    first, last = pos == 0, pos == period - 1

    def conv(part):
        cols = slice(part * c, (part + 1) * c)
        p = jnp.dot(h_scr[...], why_ref[:, cols], preferred_element_type=F32)
        prev = jnp.where(first, 0.0, pltpu.roll(p, 1, axis=0))
        nxt = jnp.where(last, 0.0, pltpu.roll(p, tm - 1, axis=0))
        return prev * cw_ref[0:1, cols] + p * cw_ref[1:2, cols] + nxt * cw_ref[2:3, cols] + cb_ref[:, cols]

    z_ref[...] = (conv(1) * conv(2)).astype(BF16)
    x0_ref[...] = conv(0).astype(BF16)
    lr_ref[...] = jnp.dot(h_scr[...], wlr_ref[...], preferred_element_type=F32)
    n = wm_ref.shape[1]
    for j0 in range(0, n, MAIN_CHUNK):
        cols = slice(j0, min(j0 + MAIN_CHUNK, n))
        p_ref[:, cols] = jnp.dot(h_scr[...], wm_ref[:, cols], preferred_element_type=F32).astype(BF16)


def _in_proj(x2, mod_l, norm_g, layer, w_hy, conv_w, conv_b, w_main, w_lr, rows_per_cond, cond_base, period):
    r, d = x2.shape
    c = w_hy.shape[2] // 3
    n = w_main.shape[2]
    tm = _tile(min(r, rows_per_cond), 512)
    assert tm % period == 0
    blocks_per_cond = rows_per_cond // tm
    resident = lambda a: pl.BlockSpec((None,) + a.shape[1:], lambda i: (layer,) + (0,) * (a.ndim - 1),
                                      pipeline_mode=pl.Buffered(1))
    rows = lambda w: pl.BlockSpec((tm, w), lambda i: (i, 0))
    return pl.pallas_call(
        functools.partial(_in_proj_body, d=d, c=c, period=period),
        grid=(r // tm,),
        in_specs=[
            rows(d),
            pl.BlockSpec((None, 1, N_MOD * d), lambda i: (cond_base + i // blocks_per_cond, 0, 0)),
            pl.BlockSpec((1, d), lambda i: (0, 0)),
            resident(w_hy), resident(conv_w), resident(conv_b), resident(w_main), resident(w_lr),
        ],
        out_specs=[rows(c), rows(c), rows(n), rows(LANES)],
        out_shape=[jax.ShapeDtypeStruct((r, c), BF16), jax.ShapeDtypeStruct((r, c), BF16),
                   jax.ShapeDtypeStruct((r, n), BF16), jax.ShapeDtypeStruct((r, LANES), F32)],
        scratch_shapes=[pltpu.VMEM((tm, d), BF16)],
        compiler_params=_cparams("parallel"),
        name="in_proj",
    )(x2, mod_l, norm_g, w_hy, conv_w, conv_b, w_main, w_lr)


def _filter_body(feat_ref, w1_ref, b1_ref, w2_ref, b2_ref, w3_ref, fr_ref, dec_ref, fb_ref, *, c, tl):
    i = pl.program_id(0)
    hp = lax.Precision.HIGHEST
    feat = feat_ref[...]
    fr = fr_ref[...]
    a = jnp.sin(fr * (jnp.dot(feat, w1_ref[...], precision=hp, preferred_element_type=F32) + b1_ref[...]))
    a = jnp.sin(fr * (jnp.dot(a, w2_ref[...], precision=hp, preferred_element_type=F32) + b2_ref[...]))
    hk = jnp.dot(a, w3_ref[...], precision=hp, preferred_element_type=F32)
    t = feat[:, 0:1]
    window = jnp.exp(-t * jnp.abs(dec_ref[...]))
    h_f = hk[:, :c] * window
    h_b = hk[:, c:] * window
    row = i * tl + lax.broadcasted_iota(jnp.int32, (tl, 1), 0)
    h_b = jnp.where(row == 0, 0.0, h_b)
    fb_ref[:, :c] = h_f.astype(BF16)
    fb_ref[:, c:] = h_b.astype(BF16)


def _pad_to(a, shape):
    return jnp.pad(a, [(0, s - n) for n, s in zip(a.shape, shape)])


def _hyena_filter_parts(feat, w1, b1, w2, b2, w3, freq, decay):
    l = feat.shape[0]
    c = decay.shape[-1]
    fe = fo = LANES
    assert feat.shape[1] <= fe and w1.shape[1] <= fo
    feat = _pad_to(feat, (l, fe))
    w1, w2, w3 = _pad_to(w1, (fe, fo)), _pad_to(w2, (fo, fo)), _pad_to(w3, (fo, 2 * c))
    b1, b2, freq = _pad_to(b1, (fo,)), _pad_to(b2, (fo,)), _pad_to(freq, (fo,))
    tl = _tile(l, 512)
    full = lambda shape: pl.BlockSpec(shape, lambda i: (0, 0))
    return pl.pallas_call(
        functools.partial(_filter_body, c=c, tl=tl),
        grid=(l // tl,),
        in_specs=[
            pl.BlockSpec((tl, fe), lambda i: (i, 0)),
            full((fe, fo)), full((1, fo)), full((fo, fo)), full((1, fo)), full((fo, 2 * c)),
            full((1, fo)), full((1, c)),
        ],
        out_specs=pl.BlockSpec((tl, 2 * c), lambda i: (i, 0)),
        out_shape=jax.ShapeDtypeStruct((l, 2 * c), BF16),
        compiler_params=_cparams("parallel"),
        name="hyena_filter",
    )(feat, w1, b1.reshape(1, fo), w2, b2.reshape(1, fo), w3, freq.reshape(1, fo), decay.reshape(1, c))


def _filter_features(l):
    t = jnp.linspace(0.0, 1.0, l, dtype=F32)[:, None]
    ang = (2.0 * math.pi / l) * jnp.arange(l, dtype=F32)[:, None] * \
        jnp.linspace(1e-4, N_BANDS - 1, N_BANDS, dtype=F32)[None, :]
    return jnp.concatenate([t, jnp.cos(ang), -jnp.sin(ang)], axis=-1)


DFT_BLOCK = 256
BF16_SUBLANES = 16


def _dft_plan(l):
    nb = min(DFT_BLOCK, l)
    nslab = l // nb
    grp = max(BF16_SUBLANES, LANES // nslab)
    assert l % nb == 0 and nb % grp == 0
    return nb, nslab, grp


def _dft_constants(l):
    nb, nslab, grp = _dft_plan(l)
    n = 2 * l
    nk1 = nslab + 1
    ngrp = nb // grp
    gi = np.arange(ngrp)[:, None, None]
    row = np.arange(nk1 * 2 * grp)[None, :, None]
    col = np.arange(nslab * grp)[None, None, :]
    k1, part, a_out = row // (2 * grp), (row // grp) % 2, row % grp
    n1, a_in = col // grp, col % grp
    t = nb * n1 + grp * gi + a_in
    ang = ((t * k1) % n) * (2.0 * math.pi / n)
    f1 = np.where(a_out == a_in, np.where(part == 0, np.cos(ang), -np.sin(ang)), 0.0)
    weight = np.where((k1 == 0) | (k1 == nslab), 1.0 / n, 2.0 / n)
    f1i = np.swapaxes(f1 * weight, 1, 2)
    r2 = np.arange(nb)
    th = ((r2[:, None] * r2[None, :]) % nb) * (2.0 * math.pi / nb)
    cs, sn = np.cos(th), np.sin(th)
    f2 = np.block([[cs, sn], [-sn, cs]])
    return tuple(jnp.asarray(m, F32).astype(BF16) for m in (f1, f1i, f2, f2.T))


def _unrolled_loop(n, width, fn):
    def body(i, carry):
        for j in range(width):
            fn(i * width + j)
        return carry

    if n >= width:
        lax.fori_loop(0, n // width, body, 0)
    for i in range(n - n % width, n):
        fn(i)


def _dft_stage1(srcs, f1_ref, t_scr, nb, nslab, grp):
    nk1 = nslab + 1
    lanes = t_scr.shape[-1]

    def group(gi):
        r0 = pl.multiple_of(gi * grp, grp)
        data = jnp.concatenate([s[:, pl.ds(r0, grp), :].reshape(nslab * grp, -1) for s in srcs], axis=1)
        t = jnp.dot(f1_ref[gi], data, preferred_element_type=F32)
        t_scr[:, :, pl.ds(r0, grp), :] = t.astype(BF16).reshape(nk1, 2, grp, lanes)

    _unrolled_loop(nb // grp, 4, group)


def _spectrum_body(hf_ref, hb_ref, f1_ref, f2_ref, h_ref, t_scr, *, nb, nslab, grp):
    ct = hf_ref.shape[-1]
    _dft_stage1([hf_ref, hb_ref], f1_ref, t_scr, nb, nslab, grp)

    def residue(k1):
        x = jnp.dot(f2_ref[...], t_scr[k1].reshape(2 * nb, 2 * ct), preferred_element_type=F32)
        h_ref[k1, :nb, :] = x[:nb, :ct] + x[:nb, ct:]
        h_ref[k1, nb:, :] = x[nb:, :ct] - x[nb:, ct:]

    _unrolled_loop(nslab + 1, 2, residue)


def _filter_spectrum(fb, consts, l):
    nb, nslab, grp = _dft_plan(l)
    f1, _, f2, _ = consts
    c = fb.shape[1] // 2
    ct = _tile(c, 256)
    ncj = c // ct
    nk1 = nslab + 1
    const = lambda a: pl.BlockSpec(a.shape, lambda j: (0,) * a.ndim, pipeline_mode=pl.Buffered(1))
    taps = lambda half: pl.BlockSpec((nslab, nb, ct), lambda j: (0, 0, half * ncj + j))
    fb3 = fb.reshape(nslab, nb, 2 * c)
    return pl.pallas_call(
        functools.partial(_spectrum_body, nb=nb, nslab=nslab, grp=grp),
        grid=(ncj,),
        in_specs=[taps(0), taps(1), const(f1), const(f2)],
        out_specs=pl.BlockSpec((nk1, 2 * nb, ct), lambda j: (0, 0, j)),
        out_shape=jax.ShapeDtypeStruct((nk1, 2 * nb, c), F32),
        scratch_shapes=[pltpu.VMEM((nk1, 2, nb, 2 * ct), BF16)],
        compiler_params=_cparams("parallel"),
        name="filter_spectrum",
    )(fb3, fb3, f1, f2)


def _long_conv_body(z_ref, h_ref, f1_ref, f1i_ref, f2_ref, f2i_ref, r_ref, t_scr, *, nb, nslab, grp):
    nseq, ct = z_ref.shape[0], z_ref.shape[-1]
    lanes = nseq * ct
    nk1 = nslab + 1
    _dft_stage1([z_ref.at[b] for b in range(nseq)], f1_ref, t_scr, nb, nslab, grp)

    def spectral(k1):
        x = jnp.dot(f2_ref[...], t_scr[k1].reshape(2 * nb, lanes), preferred_element_type=F32)
        xr, xi = x[:nb], x[nb:]
        hr = jnp.concatenate([h_ref[k1, :nb, :]] * nseq, axis=1)
        hi = jnp.concatenate([h_ref[k1, nb:, :]] * nseq, axis=1)
        y = jnp.concatenate([xr * hr - xi * hi, xr * hi + xi * hr], axis=0).astype(BF16)
        u = jnp.dot(f2i_ref[...], y, preferred_element_type=F32)
        t_scr[k1] = u.astype(BF16).reshape(2, nb, lanes)

    _unrolled_loop(nk1, 4, spectral)

    def inverse1(gi):
        r0 = pl.multiple_of(gi * grp, grp)
        data = t_scr[:, :, pl.ds(r0, grp), :].reshape(nk1 * 2 * grp, lanes)
        out = jnp.dot(f1i_ref[gi], data, preferred_element_type=F32).astype(r_ref.dtype)
        for b in range(nseq):
            r_ref[b, :, pl.ds(r0, grp), :] = out[:, b * ct:(b + 1) * ct].reshape(nslab, grp, ct)

    _unrolled_loop(nb // grp, 4, inverse1)


def _long_conv(zb, hspec, consts, l):
    nb, nslab, grp = _dft_plan(l)
    f1, f1i, f2, f2i = consts
    rows, c = zb.shape
    nbatch = rows // l
    nseq = 2 if nbatch % 2 == 0 else 1
    ct = _tile(c, 256)
    nk1 = nslab + 1
    single = pl.Buffered(1)
    const = lambda a: pl.BlockSpec(a.shape, lambda j, b: (0,) * a.ndim, pipeline_mode=single)
    seq_spec = pl.BlockSpec((nseq, nslab, nb, ct), lambda j, b: (b, 0, 0, j))
    out = pl.pallas_call(
        functools.partial(_long_conv_body, nb=nb, nslab=nslab, grp=grp),
        grid=(c // ct, nbatch // nseq),
        in_specs=[seq_spec, pl.BlockSpec((nk1, 2 * nb, ct), lambda j, b: (0, 0, j), pipeline_mode=single),
                  const(f1), const(f1i), const(f2), const(f2i)],
        out_specs=seq_spec,
        out_shape=jax.ShapeDtypeStruct((nbatch, nslab, nb, c), BF16),
        scratch_shapes=[pltpu.VMEM((nk1, 2, nb, nseq * ct), BF16)],
        compiler_params=_cparams("parallel", "parallel"),
        name="long_conv",
    )(zb.reshape(nbatch, nslab, nb, c), hspec, f1, f1i, f2, f2i)
    return out.reshape(rows, c)


def _split_bf16(x):
    hi = x.astype(BF16)
    lo = (x - hi.astype(F32)).astype(BF16)
    return hi, lo


def _gla_body(qf_ref, kf_ref, vf_ref, lrf_ref, qb_ref, kb_ref, vb_ref, lrb_ref, gw_ref, gb_ref,
              s0f_ref, s0b_ref, of_ref, ob_ref, sf_ref, sb_ref, *, dk, dv, gsz):
    step = pl.program_id(1)
    ck = GLA_CHUNK
    tb = gsz * ck

    @pl.when(step == 0)
    def _():
        sf_ref[...] = s0f_ref[...]
        sb_ref[...] = s0b_ref[...]

    rows = lax.broadcasted_iota(jnp.int32, (tb, tb), 0)
    cols = lax.broadcasted_iota(jnp.int32, (tb, tb), 1)
    same_chunk = (rows // ck) == (cols // ck)
    scale = dk ** -0.5

    def direction(q_ref, k_ref, v_ref, lr_ref, o_ref, s_ref, d):
        fwd = d == 0
        g = _log_sigmoid(jnp.dot(lr_ref[...].astype(BF16), gw_ref[d], preferred_element_type=F32)
                         + gb_ref[d]) / GATE_TAU
        keep = same_chunk & ((cols <= rows) if fwd else (cols >= rows))
        tri = keep.astype(BF16)
        g_hi, g_lo = _split_bf16(g)
        bcum = jnp.dot(tri, g_hi, preferred_element_type=F32) + jnp.dot(tri, g_lo, preferred_element_type=F32)
        ends = [bcum[(c * ck + ck - 1 if fwd else c * ck):(c * ck + ck if fwd else c * ck + 1), :] for c in range(gsz)]
        pad = [jnp.zeros_like(ends[0])] * (SUBLANES - gsz)
        decay = jnp.exp(jnp.concatenate(ends + pad, axis=0))
        decay_t = decay.T
        decay_rows = jnp.concatenate([jnp.broadcast_to(decay[c:c + 1], (ck, decay.shape[1])) for c in range(gsz)], axis=0)
        q = q_ref[...].astype(F32) * scale
        k_e = k_ref[...].astype(F32) * jnp.exp(-bcum)
        q_t = (q * jnp.exp(bcum)).astype(BF16)
        k_t = k_e.astype(BF16)
        k_d = (k_e * decay_rows).astype(BF16)
        order = range(gsz) if fwd else range(gsz - 1, -1, -1)
        for h in range(GLA_HEADS):
            kc = slice(h * dk, (h + 1) * dk)
            vc = slice(h * dv, (h + 1) * dv)
            v = v_ref[:, vc]
            att = lax.dot_general(q_t[:, kc], k_t[:, kc], (((1,), (1,)), ((), ())), preferred_element_type=F32)
            o_intra = jnp.dot(jnp.where(keep, att, 0.0).astype(BF16), v, preferred_element_type=F32)
            s = s_ref[h]
            for c in order:
                rs = slice(c * ck, (c + 1) * ck)
                o = o_intra[rs] + jnp.dot(q_t[rs, kc], s.astype(BF16), preferred_element_type=F32)
                o_ref[rs, vc] = o.astype(o_ref.dtype)
                upd = lax.dot_general(k_d[rs, kc], v[rs], (((0,), (0,)), ((), ())), preferred_element_type=F32)
                s = decay_t[kc, c:c + 1] * s + upd
            s_ref[h] = s

    direction(qf_ref, kf_ref, vf_ref, lrf_ref, of_ref, sf_ref, 0)
    direction(qb_ref, kb_ref, vb_ref, lrb_ref, ob_ref, sb_ref, 1)


def _gla_scan(p, lr, gate_w, gate_b, s0_f, s0_b, seq_len, cols):
    r = p.shape[0]
    nb = r // seq_len
    dkt = gate_w.shape[2]
    dk = dkt // GLA_HEADS
    dv = s0_f.shape[-1]
    dvt = dv * GLA_HEADS
    gsz = _tile(seq_len // GLA_CHUNK, 4)
    assert gsz <= SUBLANES
    tb = gsz * GLA_CHUNK
    n = seq_len // tb
    q0, k0, v0 = cols
    fwd_row = lambda b, i: b * n + i
    bwd_row = lambda b, i: b * n + (n - 1 - i)

    def specs(row):
        return [
            pl.BlockSpec((tb, dkt), lambda b, i: (row(b, i), q0 // dkt)),
            pl.BlockSpec((tb, dkt), lambda b, i: (row(b, i), k0 // dkt)),
            pl.BlockSpec((tb, dvt), lambda b, i: (row(b, i), v0 // dvt)),
            pl.BlockSpec((tb, LANES), lambda b, i: (row(b, i), 0)),
        ]

    state_spec = pl.BlockSpec((None, GLA_HEADS, dk, dv), lambda b, i: (b, 0, 0, 0))
    return pl.pallas_call(
        functools.partial(_gla_body, dk=dk, dv=dv, gsz=gsz),
        grid=(nb, n),
        in_specs=specs(fwd_row) + specs(bwd_row) + [
            pl.BlockSpec((2, LANES, dkt), lambda b, i: (0, 0, 0)),
            pl.BlockSpec((2, 1, dkt), lambda b, i: (0, 0, 0)),
            state_spec, state_spec,
        ],
        out_specs=[
            pl.BlockSpec((tb, dvt), lambda b, i: (fwd_row(b, i), 0)),
            pl.BlockSpec((tb, dvt), lambda b, i: (bwd_row(b, i), 0)),
            state_spec, state_spec,
        ],
        out_shape=[jax.ShapeDtypeStruct((r, dvt), BF16), jax.ShapeDtypeStruct((r, dvt), BF16),
                   jax.ShapeDtypeStruct(s0_f.shape, F32), jax.ShapeDtypeStruct(s0_b.shape, F32)],
        compiler_params=_cparams("parallel", "arbitrary"),
        name="gla_scan",
    )(p, p, p, lr, p, p, p, lr, gate_w, gate_b.reshape(2, 1, dkt), s0_f, s0_b)


def _mix_out_body(x_ref, mod_ref, x0_ref, z_ref, r_ref, of_ref, ob_ref, rg_ref, gh_ref, gg_ref,
                  skip_ref, gn_ref, wh_ref, wg_ref, wo_ref, o_ref, *, d, dv):
    f32 = lambda ref: ref[...].astype(F32)
    y_hy = f32(x0_ref) * (f32(r_ref) + skip_ref[...] * f32(z_ref))
    o = f32(of_ref) + f32(ob_ref)
    rg = f32(rg_ref)
    parts = []
    for h in range(GLA_HEADS):
        oh = o[:, h * dv:(h + 1) * dv]
        on = oh * lax.rsqrt(jnp.mean(oh * oh, axis=-1, keepdims=True) + EPS) * gn_ref[...]
        parts.append((on * _silu(rg[:, h * dv:(h + 1) * dv])).astype(BF16))
    y_gla = jnp.concatenate(parts, axis=-1)
    ph = jnp.dot(y_hy.astype(BF16), wh_ref[...], preferred_element_type=F32)
    pg = jnp.dot(y_gla, wg_ref[...], preferred_element_type=F32)
    merged = _sigmoid(f32(gh_ref)) * ph + _sigmoid(f32(gg_ref)) * pg
    out = jnp.dot(merged.astype(BF16), wo_ref[...], preferred_element_type=F32)
    g1 = mod_ref[:, 2 * d:3 * d]
    o_ref[...] = x_ref[...] + g1 * out


def _mix_out(x2, mod_l, x0, z, rconv, o_f, o_b, p, cols, skip, gn, layer, w_bhy, w_bgla, w_o,
             rows_per_cond, cond_base):
    r, d = x2.shape
    c = x0.shape[1]
    dvt = o_f.shape[1]
    dv = dvt // GLA_HEADS
    rg0, gh0, gg0 = cols
    tm = _tile(min(r, rows_per_cond), 512)
    bpc = rows_per_cond // tm
    row = lambda i: (i, 0)
    full = lambda shape: pl.BlockSpec(shape, lambda i: (0, 0))
    resident = lambda a: pl.BlockSpec((None,) + a.shape[1:], lambda i: (layer, 0, 0), pipeline_mode=pl.Buffered(1))
    return pl.pallas_call(
        functools.partial(_mix_out_body, d=d, dv=dv),
        grid=(r // tm,),
        in_specs=[
            pl.BlockSpec((tm, d), row),
            pl.BlockSpec((None, 1, N_MOD * d), lambda i: (cond_base + i // bpc, 0, 0)),
            pl.BlockSpec((tm, c), row),
            pl.BlockSpec((tm, c), row),
            pl.BlockSpec((tm, c), row),
            pl.BlockSpec((tm, dvt), row),
            pl.BlockSpec((tm, dvt), row),
            pl.BlockSpec((tm, dvt), lambda i: (i, rg0 // dvt)),
            pl.BlockSpec((tm, d), lambda i: (i, gh0 // d)),
            pl.BlockSpec((tm, d), lambda i: (i, gg0 // d)),
            full((1, c)), full((1, dv)), resident(w_bhy), resident(w_bgla), resident(w_o),
        ],
        out_specs=pl.BlockSpec((tm, d), row),
        out_shape=jax.ShapeDtypeStruct((r, d), F32),
        compiler_params=_cparams("parallel"),
        name="mix_out",
    )(x2, mod_l, x0, z, rconv, o_f, o_b, p, p, p, skip, gn, w_bhy, w_bgla, w_o)


FF_CHUNK = 1024


def _mlp_body(x_ref, mod_ref, g_ref, w1_ref, w2_ref, fg_ref, o_ref, h_scr, *, d, final_norm):
    h_scr[...] = _modulated_norm(x_ref[...], g_ref[...], mod_ref[...], 3, 4, d).astype(BF16)
    dff = w1_ref.shape[1]
    acc = None
    for f0 in range(0, dff, FF_CHUNK):
        ff = slice(f0, min(f0 + FF_CHUNK, dff))
        a = jnp.maximum(jnp.dot(h_scr[...], w1_ref[:, ff], preferred_element_type=F32), 0.0)
        t = jnp.dot((a * a).astype(BF16), w2_ref[ff, :], preferred_element_type=F32)
        acc = t if acc is None else acc + t
    g2 = mod_ref[:, 5 * d:6 * d]
    y = x_ref[...] + g2 * acc
    if final_norm:
        y = y * lax.rsqrt(jnp.mean(y * y, axis=-1, keepdims=True) + EPS) * fg_ref[...]
    o_ref[...] = y


def _mlp(x2, mod_l, norm_g, layer, w1, w2, final_g, rows_per_cond, cond_base, final_norm):
    r, d = x2.shape
    tm = _tile(min(r, rows_per_cond), 512)
    bpc = rows_per_cond // tm
    resident = lambda a: pl.BlockSpec((None,) + a.shape[1:], lambda i: (layer, 0, 0), pipeline_mode=pl.Buffered(1))
    return pl.pallas_call(
        functools.partial(_mlp_body, d=d, final_norm=final_norm),
        grid=(r // tm,),
        in_specs=[
            pl.BlockSpec((tm, d), lambda i: (i, 0)),
            pl.BlockSpec((None, 1, N_MOD * d), lambda i: (cond_base + i // bpc, 0, 0)),
            pl.BlockSpec((1, d), lambda i: (0, 0)),
            resident(w1), resident(w2),
            pl.BlockSpec((1, d), lambda i: (0, 0)),
        ],
        out_specs=pl.BlockSpec((tm, d), lambda i: (i, 0)),
        out_shape=jax.ShapeDtypeStruct((r, d), F32),
        scratch_shapes=[pltpu.VMEM((tm, d), BF16)],
        compiler_params=_cparams("parallel"),
        name="mlp",
    )(x2, mod_l, norm_g, w1, w2, final_g)


def kernel(x, c, ctx, c_ctx, ada_w, ada_b, norm1_g, norm2_g, w_in, hy_conv_w, hy_conv_b, hy_filt_w1, hy_filt_b1, hy_filt_w2, hy_filt_b2, hy_filt_w3, hy_filt_freq, hy_decay, hy_skip, gla_gate_w, gla_gate_b, gla_norm_g, w_branch_hy, w_branch_gla, w_out, mlp_w1, mlp_w2, final_g):
    nb, seq, d = x.shape
    lctx = ctx.shape[1]
    depth = ada_w.shape[0]
    chy = hy_decay.shape[-1]
    rank = gla_gate_w.shape[2]
    dkt = gla_gate_w.shape[3]
    dvt = w_branch_gla.shape[1]
    dk, dv = dkt // GLA_HEADS, dvt // GLA_HEADS
    assert nb + 1 <= COND_ROWS and 2 * rank <= LANES

    cond = jnp.zeros((COND_ROWS, d), F32).at[:nb].set(c).at[nb].set(c_ctx)
    mod = _ada_mod(cond, ada_w, ada_b).reshape(depth, COND_ROWS, 1, N_MOD * d)

    sizes = (3 * chy, dkt, dkt, dvt, dvt, rank, rank, d, d)
    offs = np.concatenate([[0], np.cumsum(sizes)])
    w_hy = w_in[..., :offs[1]].astype(BF16)
    w_main = jnp.concatenate([w_in[..., offs[1]:offs[5]], w_in[..., offs[7]:]], axis=-1).astype(BF16)
    w_lr = jnp.pad(w_in[..., offs[5]:offs[7]], ((0, 0), (0, 0), (0, LANES - 2 * rank))).astype(BF16)
    conv_b = hy_conv_b.reshape(depth, 1, 3 * chy)
    q0 = 0
    k0 = q0 + dkt
    v0 = k0 + dkt
    rg0 = v0 + dvt
    gh0 = rg0 + dvt
    gg0 = gh0 + d

    gate_w_pad = jnp.zeros((depth, 2, LANES, dkt), F32)
    gate_w_pad = gate_w_pad.at[:, 0, :rank].set(gla_gate_w[:, 0]).at[:, 1, rank:2 * rank].set(gla_gate_w[:, 1])
    gate_w_pad = gate_w_pad.astype(BF16)

    w_bhy = w_branch_hy.astype(BF16)
    w_bgla = w_branch_gla.astype(BF16)
    w_o = w_out.astype(BF16)
    w1 = mlp_w1.astype(BF16)
    w2 = mlp_w2.astype(BF16)

    dft_lat, dft_ctx = _dft_constants(seq), _dft_constants(lctx)
    feat_lat, feat_ctx = _filter_features(seq), _filter_features(lctx)
    zero_state = jnp.zeros((nb, GLA_HEADS, dk, dv), F32)

    xs = x.reshape(nb * seq, d)
    cs = ctx.reshape(nb * lctx, d)
    for l in range(depth):
        last = l == depth - 1
        mod_l = mod[l]
        n1 = norm1_g[l].reshape(1, d)
        filt = (hy_filt_w1[l], hy_filt_b1[l], hy_filt_w2[l], hy_filt_b2[l], hy_filt_w3[l],
                hy_filt_freq[l], hy_decay[l])
        skip = hy_skip[l].reshape(1, chy)
        gn = gla_norm_g[l].reshape(1, dv)

        in_w = (l, w_hy, hy_conv_w, conv_b, w_main, w_lr)
        out_w = (l, w_bhy, w_bgla, w_o)
        x0_c, z_c, p_c, lr_c = _in_proj(cs, mod_l, n1, *in_w, nb * lctx, nb, lctx)
        of_c, ob_c, sf_c, sb_c = _gla_scan(p_c, lr_c, gate_w_pad[l], gla_gate_b[l], zero_state, zero_state,
                                           lctx, (q0, k0, v0))
        x0_l, z_l, p_l, lr_l = _in_proj(xs, mod_l, n1, *in_w, seq, 0, GRID_W)
        of_l, ob_l, _, _ = _gla_scan(p_l, lr_l, gate_w_pad[l], gla_gate_b[l], sf_c, sb_c, seq, (q0, k0, v0))

        h_l = _filter_spectrum(_hyena_filter_parts(feat_lat, *filt), dft_lat, seq)
        r_l = _long_conv(z_l, h_l, dft_lat, seq)
        xs = _mix_out(xs, mod_l, x0_l, z_l, r_l, of_l, ob_l, p_l, (rg0, gh0, gg0), skip, gn, *out_w, seq, 0)
        if not last:
            h_c = _filter_spectrum(_hyena_filter_parts(feat_ctx, *filt), dft_ctx, lctx)
            r_c = _long_conv(z_c, h_c, dft_ctx, lctx)
            cs = _mix_out(cs, mod_l, x0_c, z_c, r_c, of_c, ob_c, p_c, (rg0, gh0, gg0), skip, gn, *out_w,
                          nb * lctx, nb)

        n2 = norm2_g[l].reshape(1, d)
        fg = final_g.reshape(1, d)
        xs = _mlp(xs, mod_l, n2, l, w1, w2, fg, seq, 0, last)
        if not last:
            cs = _mlp(cs, mod_l, n2, l, w1, w2, fg, nb * lctx, nb, False)
    return xs.reshape(nb, seq, d)
```

```python
import functools
import math

import numpy as np
import jax
import jax.numpy as jnp
from jax import lax
from jax.experimental import pallas as pl
from jax.experimental.pallas import tpu as pltpu

F32 = jnp.float32
BF16 = jnp.bfloat16

GRID_W = 64
N_BANDS = 16
GLA_HEADS = 4
GLA_CHUNK = 64
GATE_TAU = 16.0
EPS = 1e-6
N_MOD = 6

V7X_VMEM_BYTES = 64 * 1024 * 1024
VMEM_LIMIT_BYTES = V7X_VMEM_BYTES * 3 // 4
LANES = 128
SUBLANES = 8
COND_ROWS = SUBLANES


def _cparams(*sem):
    return pltpu.CompilerParams(dimension_semantics=sem, vmem_limit_bytes=VMEM_LIMIT_BYTES)


def _tile(n, pref):
    t = min(n, pref)
    while n % t:
        t //= 2
    return t


def _sigmoid(x):
    return 1.0 / (1.0 + jnp.exp(-x))


def _silu(x):
    return x * _sigmoid(x)


def _log_sigmoid(x):
    return jnp.minimum(x, 0.0) - jnp.log(1.0 + jnp.exp(-jnp.abs(x)))


def _modulated_norm(x, g, mod, shift_idx, scale_idx, d):
    y = x * lax.rsqrt(jnp.mean(x * x, axis=-1, keepdims=True) + EPS) * g
    shift = mod[:, shift_idx * d:(shift_idx + 1) * d]
    scale = mod[:, scale_idx * d:(scale_idx + 1) * d]
    return y * (1.0 + scale) + shift


def _ada_body(c_ref, w_ref, b_ref, o_ref):
    s = _silu(c_ref[...]).astype(BF16)
    o_ref[...] = jnp.dot(s, w_ref[...].astype(BF16), preferred_element_type=F32) + b_ref[...]


def _ada_mod(cond, ada_w, ada_b):
    depth, d, n = ada_w.shape
    tn = _tile(n, 1536)
    return pl.pallas_call(
        _ada_body,
        grid=(depth, n // tn),
        in_specs=[
            pl.BlockSpec((COND_ROWS, d), lambda l, j: (0, 0)),
            pl.BlockSpec((None, d, tn), lambda l, j: (l, 0, j)),
            pl.BlockSpec((None, 1, tn), lambda l, j: (l, 0, j)),
        ],
        out_specs=pl.BlockSpec((None, COND_ROWS, tn), lambda l, j: (l, 0, j)),
        out_shape=jax.ShapeDtypeStruct((depth, COND_ROWS, n), F32),
        compiler_params=_cparams("parallel", "parallel"),
        name="ada_mod",
    )(cond, ada_w, ada_b.reshape(depth, 1, n))


MAIN_CHUNK = 1024
HYENA_CHUNK = 512


def _in_proj_body(x_ref, mod_ref, g_ref, why_ref, cw_ref, cb_ref, wm_ref, wlr_ref,
                  x0_ref, z_ref, p_ref, lr_ref, h_scr, *, d, c, period):
    h_scr[...] = _modulated_norm(x_ref[...], g_ref[...], mod_ref[...], 0, 1, d).astype(BF16)
    tm = x_ref.shape[0]
    pos = lax.broadcasted_iota(jnp.int32, (tm, 1), 0) % period
    first, last = pos == 0, pos == period - 1

    def conv(part, ch):
        cols = slice(part * c + ch.start, part * c + ch.stop)
        p = jnp.dot(h_scr[...], why_ref[:, cols], preferred_element_type=F32)
        prev = jnp.where(first, 0.0, pltpu.roll(p, 1, axis=0))
        nxt = jnp.where(last, 0.0, pltpu.roll(p, tm - 1, axis=0))
        return prev * cw_ref[0:1, cols] + p * cw_ref[1:2, cols] + nxt * cw_ref[2:3, cols] + cb_ref[:, cols]

    n = wm_ref.shape[1]
    main_chunks = [slice(j0, min(j0 + MAIN_CHUNK, n)) for j0 in range(0, n, MAIN_CHUNK)]
    hy_chunks = [slice(j0, min(j0 + HYENA_CHUNK, c)) for j0 in range(0, c, HYENA_CHUNK)]

    def main():
        if main_chunks:
            cols = main_chunks.pop(0)
            p_ref[:, cols] = jnp.dot(h_scr[...], wm_ref[:, cols], preferred_element_type=F32).astype(BF16)

    for ch in hy_chunks:
        x1 = conv(1, ch)
        main()
        z_ref[:, ch] = (x1 * conv(2, ch)).astype(BF16)
    for ch in hy_chunks:
        x0_ref[:, ch] = conv(0, ch).astype(BF16)
        main()
    while main_chunks:
        main()
    lr_ref[...] = jnp.dot(h_scr[...], wlr_ref[...], preferred_element_type=F32)


def _in_proj(x2, mod_l, norm_g, layer, w_hy, conv_w, conv_b, w_main, w_lr, rows_per_cond, cond_base, period):
    r, d = x2.shape
    c = w_hy.shape[2] // 3
    n = w_main.shape[2]
    tm = _tile(min(r, rows_per_cond), 512)
    assert tm % period == 0
    blocks_per_cond = rows_per_cond // tm
    resident = lambda a: pl.BlockSpec((None,) + a.shape[1:], lambda i: (layer,) + (0,) * (a.ndim - 1),
                                      pipeline_mode=pl.Buffered(1))
    rows = lambda w: pl.BlockSpec((tm, w), lambda i: (i, 0))
    return pl.pallas_call(
        functools.partial(_in_proj_body, d=d, c=c, period=period),
        grid=(r // tm,),
        in_specs=[
            rows(d),
            pl.BlockSpec((None, 1, N_MOD * d), lambda i: (cond_base + i // blocks_per_cond, 0, 0)),
            pl.BlockSpec((1, d), lambda i: (0, 0)),
            resident(w_hy), resident(conv_w), resident(conv_b), resident(w_main), resident(w_lr),
        ],
        out_specs=[rows(c), rows(c), rows(n), rows(LANES)],
        out_shape=[jax.ShapeDtypeStruct((r, c), BF16), jax.ShapeDtypeStruct((r, c), BF16),
                   jax.ShapeDtypeStruct((r, n), BF16), jax.ShapeDtypeStruct((r, LANES), F32)],
        scratch_shapes=[pltpu.VMEM((tm, d), BF16)],
        compiler_params=_cparams("parallel"),
        name="in_proj",
    )(x2, mod_l, norm_g, w_hy, conv_w, conv_b, w_main, w_lr)


def _filter_body(feat_ref, w1_ref, b1_ref, w2_ref, b2_ref, w3_ref, fr_ref, dec_ref, fb_ref, *, c, tl):
    i = pl.program_id(0)
    hp = lax.Precision.HIGHEST
    feat = feat_ref[...]
    fr = fr_ref[...]
    a = jnp.sin(fr * (jnp.dot(feat, w1_ref[...], precision=hp, preferred_element_type=F32) + b1_ref[...]))
    a = jnp.sin(fr * (jnp.dot(a, w2_ref[...], precision=hp, preferred_element_type=F32) + b2_ref[...]))
    hk = jnp.dot(a, w3_ref[...], precision=hp, preferred_element_type=F32)
    t = feat[:, 0:1]
    window = jnp.exp(-t * jnp.abs(dec_ref[...]))
    h_f = hk[:, :c] * window
    h_b = hk[:, c:] * window
    row = i * tl + lax.broadcasted_iota(jnp.int32, (tl, 1), 0)
    h_b = jnp.where(row == 0, 0.0, h_b)
    fb_ref[:, :c] = h_f.astype(BF16)
    fb_ref[:, c:] = h_b.astype(BF16)


def _pad_to(a, shape):
    return jnp.pad(a, [(0, s - n) for n, s in zip(a.shape, shape)])


def _hyena_filter_parts(feat, w1, b1, w2, b2, w3, freq, decay):
    l = feat.shape[0]
    c = decay.shape[-1]
    fe = fo = LANES
    assert feat.shape[1] <= fe and w1.shape[1] <= fo
    feat = _pad_to(feat, (l, fe))
    w1, w2, w3 = _pad_to(w1, (fe, fo)), _pad_to(w2, (fo, fo)), _pad_to(w3, (fo, 2 * c))
    b1, b2, freq = _pad_to(b1, (fo,)), _pad_to(b2, (fo,)), _pad_to(freq, (fo,))
    tl = _tile(l, 512)
    full = lambda shape: pl.BlockSpec(shape, lambda i: (0, 0))
    return pl.pallas_call(
        functools.partial(_filter_body, c=c, tl=tl),
        grid=(l // tl,),
        in_specs=[
            pl.BlockSpec((tl, fe), lambda i: (i, 0)),
            full((fe, fo)), full((1, fo)), full((fo, fo)), full((1, fo)), full((fo, 2 * c)),
            full((1, fo)), full((1, c)),
        ],
        out_specs=pl.BlockSpec((tl, 2 * c), lambda i: (i, 0)),
        out_shape=jax.ShapeDtypeStruct((l, 2 * c), BF16),
        compiler_params=_cparams("parallel"),
        name="hyena_filter",
    )(feat, w1, b1.reshape(1, fo), w2, b2.reshape(1, fo), w3, freq.reshape(1, fo), decay.reshape(1, c))


def _filter_features(l):
    t = jnp.linspace(0.0, 1.0, l, dtype=F32)[:, None]
    ang = (2.0 * math.pi / l) * jnp.arange(l, dtype=F32)[:, None] * \
        jnp.linspace(1e-4, N_BANDS - 1, N_BANDS, dtype=F32)[None, :]
    return jnp.concatenate([t, jnp.cos(ang), -jnp.sin(ang)], axis=-1)


DFT_BLOCK = 256
BF16_SUBLANES = 16


def _dft_plan(l):
    nb = min(DFT_BLOCK, l)
    nslab = l // nb
    grp = max(BF16_SUBLANES, LANES // nslab)
    assert l % nb == 0 and nb % grp == 0
    return nb, nslab, grp


def _dft_constants(l):
    nb, nslab, grp = _dft_plan(l)
    n = 2 * l
    nk1 = nslab + 1
    ngrp = nb // grp
    gi = np.arange(ngrp)[:, None, None]
    row = np.arange(nk1 * 2 * grp)[None, :, None]
    col = np.arange(nslab * grp)[None, None, :]
    k1, part, a_out = row // (2 * grp), (row // grp) % 2, row % grp
    n1, a_in = col // grp, col % grp
    t = nb * n1 + grp * gi + a_in
    ang = ((t * k1) % n) * (2.0 * math.pi / n)
    f1 = np.where(a_out == a_in, np.where(part == 0, np.cos(ang), -np.sin(ang)), 0.0)
    weight = np.where((k1 == 0) | (k1 == nslab), 1.0 / n, 2.0 / n)
    f1i = np.swapaxes(f1 * weight, 1, 2)
    r2 = np.arange(nb)
    th = ((r2[:, None] * r2[None, :]) % nb) * (2.0 * math.pi / nb)
    cs, sn = np.cos(th), np.sin(th)
    f2 = np.block([[cs, sn], [-sn, cs]])
    return tuple(jnp.asarray(m, F32).astype(BF16) for m in (f1, f1i, f2, f2.T))


def _unrolled_loop(n, width, fn):
    def body(i, carry):
        for j in range(width):
            fn(i * width + j)
        return carry

    if n >= width:
        lax.fori_loop(0, n // width, body, 0)
    for i in range(n - n % width, n):
        fn(i)


def _dft_stage1(srcs, f1_ref, t_scr, nb, nslab, grp):
    nk1 = nslab + 1
    lanes = t_scr.shape[-1]

    def group(gi):
        r0 = pl.multiple_of(gi * grp, grp)
        data = jnp.concatenate([s[:, pl.ds(r0, grp), :].reshape(nslab * grp, -1) for s in srcs], axis=1)
        t = jnp.dot(f1_ref[gi], data, preferred_element_type=F32)
        t_scr[:, :, pl.ds(r0, grp), :] = t.astype(BF16).reshape(nk1, 2, grp, lanes)

    _unrolled_loop(nb // grp, 4, group)


def _spectrum_body(hf_ref, hb_ref, f1_ref, f2_ref, h_ref, t_scr, *, nb, nslab, grp):
    ct = hf_ref.shape[-1]
    _dft_stage1([hf_ref, hb_ref], f1_ref, t_scr, nb, nslab, grp)

    def residue(k1):
        x = jnp.dot(f2_ref[...], t_scr[k1].reshape(2 * nb, 2 * ct), preferred_element_type=F32)
        h_ref[k1, :nb, :] = x[:nb, :ct] + x[:nb, ct:]
        h_ref[k1, nb:, :] = x[nb:, :ct] - x[nb:, ct:]

    _unrolled_loop(nslab + 1, 2, residue)


def _filter_spectrum(fb, consts, l):
    nb, nslab, grp = _dft_plan(l)
    f1, _, f2, _ = consts
    c = fb.shape[1] // 2
    ct = _tile(c, 256)
    ncj = c // ct
    nk1 = nslab + 1
    const = lambda a: pl.BlockSpec(a.shape, lambda j: (0,) * a.ndim, pipeline_mode=pl.Buffered(1))
    taps = lambda half: pl.BlockSpec((nslab, nb, ct), lambda j: (0, 0, half * ncj + j))
    fb3 = fb.reshape(nslab, nb, 2 * c)
    return pl.pallas_call(
        functools.partial(_spectrum_body, nb=nb, nslab=nslab, grp=grp),
        grid=(ncj,),
        in_specs=[taps(0), taps(1), const(f1), const(f2)],
        out_specs=pl.BlockSpec((nk1, 2 * nb, ct), lambda j: (0, 0, j)),
        out_shape=jax.ShapeDtypeStruct((nk1, 2 * nb, c), F32),
        scratch_shapes=[pltpu.VMEM((nk1, 2, nb, 2 * ct), BF16)],
        compiler_params=_cparams("parallel"),
        name="filter_spectrum",
    )(fb3, fb3, f1, f2)


def _long_conv_body(z_ref, h_ref, f1_ref, f1i_ref, f2_ref, f2i_ref, r_ref, t_scr, *, nb, nslab, grp):
    nseq, ct = z_ref.shape[0], z_ref.shape[-1]
    lanes = nseq * ct
    nk1 = nslab + 1
    _dft_stage1([z_ref.at[b] for b in range(nseq)], f1_ref, t_scr, nb, nslab, grp)

    def spectral(k1):
        x = jnp.dot(f2_ref[...], t_scr[k1].reshape(2 * nb, lanes), preferred_element_type=F32)
        xr, xi = x[:nb], x[nb:]
        hr = jnp.concatenate([h_ref[k1, :nb, :]] * nseq, axis=1)
        hi = jnp.concatenate([h_ref[k1, nb:, :]] * nseq, axis=1)
        y = jnp.concatenate([xr * hr - xi * hi, xr * hi + xi * hr], axis=0).astype(BF16)
        u = jnp.dot(f2i_ref[...], y, preferred_element_type=F32)
        t_scr[k1] = u.astype(BF16).reshape(2, nb, lanes)

    _unrolled_loop(nk1, 4, spectral)

    def inverse1(gi):
        r0 = pl.multiple_of(gi * grp, grp)
        data = t_scr[:, :, pl.ds(r0, grp), :].reshape(nk1 * 2 * grp, lanes)
        out = jnp.dot(f1i_ref[gi], data, preferred_element_type=F32).astype(r_ref.dtype)
        for b in range(nseq):
            r_ref[b, :, pl.ds(r0, grp), :] = out[:, b * ct:(b + 1) * ct].reshape(nslab, grp, ct)

    _unrolled_loop(nb // grp, 4, inverse1)


def _long_conv(zb, hspec, consts, l):
    nb, nslab, grp = _dft_plan(l)
    f1, f1i, f2, f2i = consts
    rows, c = zb.shape
    nbatch = rows // l
    nseq = 2 if nbatch % 2 == 0 else 1
    ct = _tile(c, 256)
    nk1 = nslab + 1
    single = pl.Buffered(1)
    const = lambda a: pl.BlockSpec(a.shape, lambda j, b: (0,) * a.ndim, pipeline_mode=single)
    seq_spec = pl.BlockSpec((nseq, nslab, nb, ct), lambda j, b: (b, 0, 0, j))
    out = pl.pallas_call(
        functools.partial(_long_conv_body, nb=nb, nslab=nslab, grp=grp),
        grid=(c // ct, nbatch // nseq),
        in_specs=[seq_spec, pl.BlockSpec((nk1, 2 * nb, ct), lambda j, b: (0, 0, j), pipeline_mode=single),
                  const(f1), const(f1i), const(f2), const(f2i)],
        out_specs=seq_spec,
        out_shape=jax.ShapeDtypeStruct((nbatch, nslab, nb, c), BF16),
        scratch_shapes=[pltpu.VMEM((nk1, 2, nb, nseq * ct), BF16)],
        compiler_params=_cparams("parallel", "parallel"),
        name="long_conv",
    )(zb.reshape(nbatch, nslab, nb, c), hspec, f1, f1i, f2, f2i)
    return out.reshape(rows, c)


def _split_bf16(x):
    hi = x.astype(BF16)
    lo = (x - hi.astype(F32)).astype(BF16)
    return hi, lo


def _gla_body(qf_ref, kf_ref, vf_ref, lrf_ref, qb_ref, kb_ref, vb_ref, lrb_ref, gw_ref, gb_ref,
              s0f_ref, s0b_ref, of_ref, ob_ref, sf_ref, sb_ref, *, dk, dv, gsz):
    step = pl.program_id(1)
    ck = GLA_CHUNK
    tb = gsz * ck

    @pl.when(step == 0)
    def _():
        sf_ref[...] = s0f_ref[...]
        sb_ref[...] = s0b_ref[...]

    rows = lax.broadcasted_iota(jnp.int32, (tb, tb), 0)
    cols = lax.broadcasted_iota(jnp.int32, (tb, tb), 1)
    same_chunk = (rows // ck) == (cols // ck)
    scale = dk ** -0.5

    def direction(q_ref, k_ref, v_ref, lr_ref, o_ref, s_ref, d):
        fwd = d == 0
        g = _log_sigmoid(jnp.dot(lr_ref[...].astype(BF16), gw_ref[d], preferred_element_type=F32)
                         + gb_ref[d]) / GATE_TAU
        keep = same_chunk & ((cols <= rows) if fwd else (cols >= rows))
        tri = keep.astype(BF16)
        g_hi, g_lo = _split_bf16(g)
        bcum = jnp.dot(tri, g_hi, preferred_element_type=F32) + jnp.dot(tri, g_lo, preferred_element_type=F32)
        ends = [bcum[(c * ck + ck - 1 if fwd else c * ck):(c * ck + ck if fwd else c * ck + 1), :] for c in range(gsz)]
        pad = [jnp.zeros_like(ends[0])] * (SUBLANES - gsz)
        decay = jnp.exp(jnp.concatenate(ends + pad, axis=0))
        decay_t = decay.T
        decay_rows = jnp.concatenate([jnp.broadcast_to(decay[c:c + 1], (ck, decay.shape[1])) for c in range(gsz)], axis=0)
        q = q_ref[...].astype(F32) * scale
        k_e = k_ref[...].astype(F32) * jnp.exp(-bcum)
        q_t = (q * jnp.exp(bcum)).astype(BF16)
        k_t = k_e.astype(BF16)
        k_d = (k_e * decay_rows).astype(BF16)
        order = range(gsz) if fwd else range(gsz - 1, -1, -1)
        for h in range(GLA_HEADS):
            kc = slice(h * dk, (h + 1) * dk)
            vc = slice(h * dv, (h + 1) * dv)
            v = v_ref[:, vc]
            att = lax.dot_general(q_t[:, kc], k_t[:, kc], (((1,), (1,)), ((), ())), preferred_element_type=F32)
            o_intra = jnp.dot(jnp.where(keep, att, 0.0).astype(BF16), v, preferred_element_type=F32)
            s = s_ref[h]
            for c in order:
                rs = slice(c * ck, (c + 1) * ck)
                o = o_intra[rs] + jnp.dot(q_t[rs, kc], s.astype(BF16), preferred_element_type=F32)
                o_ref[rs, vc] = o.astype(o_ref.dtype)
                upd = lax.dot_general(k_d[rs, kc], v[rs], (((0,), (0,)), ((), ())), preferred_element_type=F32)
                s = decay_t[kc, c:c + 1] * s + upd
            s_ref[h] = s

    for bi in range(qf_ref.shape[0]):
        at = lambda *refs: [r.at[bi] for r in refs]
        direction(*at(qf_ref, kf_ref, vf_ref, lrf_ref, of_ref, sf_ref), 0)
        direction(*at(qb_ref, kb_ref, vb_ref, lrb_ref, ob_ref, sb_ref), 1)


def _gla_scan(p, lr, gate_w, gate_b, s0_f, s0_b, seq_len, cols):
    r = p.shape[0]
    nb = r // seq_len
    dkt = gate_w.shape[2]
    dk = dkt // GLA_HEADS
    dv = s0_f.shape[-1]
    dvt = dv * GLA_HEADS
    gsz = _tile(seq_len // GLA_CHUNK, 4)
    assert gsz <= SUBLANES
    tb = gsz * GLA_CHUNK
    n = seq_len // tb
    nseq = 2 if nb % 2 == 0 else 1
    q0, k0, v0 = cols
    fwd_blk = lambda i: i
    bwd_blk = lambda i: n - 1 - i

    def specs(blk):
        return [
            pl.BlockSpec((nseq, tb, dkt), lambda b, i: (b, blk(i), q0 // dkt)),
            pl.BlockSpec((nseq, tb, dkt), lambda b, i: (b, blk(i), k0 // dkt)),
            pl.BlockSpec((nseq, tb, dvt), lambda b, i: (b, blk(i), v0 // dvt)),
            pl.BlockSpec((nseq, tb, LANES), lambda b, i: (b, blk(i), 0)),
        ]

    state_spec = pl.BlockSpec((nseq, GLA_HEADS, dk, dv), lambda b, i: (b, 0, 0, 0))
    p3 = p.reshape(nb, seq_len, p.shape[1])
    lr3 = lr.reshape(nb, seq_len, LANES)
    o_f, o_b, s_f, s_b = pl.pallas_call(
        functools.partial(_gla_body, dk=dk, dv=dv, gsz=gsz),
        grid=(nb // nseq, n),
        in_specs=specs(fwd_blk) + specs(bwd_blk) + [
            pl.BlockSpec((2, LANES, dkt), lambda b, i: (0, 0, 0)),
            pl.BlockSpec((2, 1, dkt), lambda b, i: (0, 0, 0)),
            state_spec, state_spec,
        ],
        out_specs=[
            pl.BlockSpec((nseq, tb, dvt), lambda b, i: (b, fwd_blk(i), 0)),
            pl.BlockSpec((nseq, tb, dvt), lambda b, i: (b, bwd_blk(i), 0)),
            state_spec, state_spec,
        ],
        out_shape=[jax.ShapeDtypeStruct((nb, seq_len, dvt), BF16), jax.ShapeDtypeStruct((nb, seq_len, dvt), BF16),
                   jax.ShapeDtypeStruct(s0_f.shape, F32), jax.ShapeDtypeStruct(s0_b.shape, F32)],
        compiler_params=_cparams("parallel", "arbitrary"),
        name="gla_scan",
    )(p3, p3, p3, lr3, p3, p3, p3, lr3, gate_w, gate_b.reshape(2, 1, dkt), s0_f, s0_b)
    return o_f.reshape(r, dvt), o_b.reshape(r, dvt), s_f, s_b


def _mix_out_body(x_ref, mod_ref, x0_ref, z_ref, r_ref, of_ref, ob_ref, rg_ref, gh_ref, gg_ref,
                  skip_ref, gn_ref, wh_ref, wg_ref, wo_ref, o_ref, *, d, dv):
    f32 = lambda ref: ref[...].astype(F32)
    y_hy = f32(x0_ref) * (f32(r_ref) + skip_ref[...] * f32(z_ref))
    o = f32(of_ref) + f32(ob_ref)
    rg = f32(rg_ref)
    parts = []
    for h in range(GLA_HEADS):
        oh = o[:, h * dv:(h + 1) * dv]
        on = oh * lax.rsqrt(jnp.mean(oh * oh, axis=-1, keepdims=True) + EPS) * gn_ref[...]
        parts.append((on * _silu(rg[:, h * dv:(h + 1) * dv])).astype(BF16))
    y_gla = jnp.concatenate(parts, axis=-1)
    ph = jnp.dot(y_hy.astype(BF16), wh_ref[...], preferred_element_type=F32)
    pg = jnp.dot(y_gla, wg_ref[...], preferred_element_type=F32)
    merged = _sigmoid(f32(gh_ref)) * ph + _sigmoid(f32(gg_ref)) * pg
    out = jnp.dot(merged.astype(BF16), wo_ref[...], preferred_element_type=F32)
    g1 = mod_ref[:, 2 * d:3 * d]
    o_ref[...] = x_ref[...] + g1 * out


def _mix_out(x2, mod_l, x0, z, rconv, o_f, o_b, p, cols, skip, gn, layer, w_bhy, w_bgla, w_o,
             rows_per_cond, cond_base):
    r, d = x2.shape
    c = x0.shape[1]
    dvt = o_f.shape[1]
    dv = dvt // GLA_HEADS
    rg0, gh0, gg0 = cols
    tm = _tile(min(r, rows_per_cond), 512)
    bpc = rows_per_cond // tm
    row = lambda i: (i, 0)
    full = lambda shape: pl.BlockSpec(shape, lambda i: (0, 0))
    resident = lambda a: pl.BlockSpec((None,) + a.shape[1:], lambda i: (layer, 0, 0), pipeline_mode=pl.Buffered(1))
    return pl.pallas_call(
        functools.partial(_mix_out_body, d=d, dv=dv),
        grid=(r // tm,),
        in_specs=[
            pl.BlockSpec((tm, d), row),
            pl.BlockSpec((None, 1, N_MOD * d), lambda i: (cond_base + i // bpc, 0, 0)),
            pl.BlockSpec((tm, c), row),
            pl.BlockSpec((tm, c), row),
            pl.BlockSpec((tm, c), row),
            pl.BlockSpec((tm, dvt), row),
            pl.BlockSpec((tm, dvt), row),
            pl.BlockSpec((tm, dvt), lambda i: (i, rg0 // dvt)),
            pl.BlockSpec((tm, d), lambda i: (i, gh0 // d)),
            pl.BlockSpec((tm, d), lambda i: (i, gg0 // d)),
            full((1, c)), full((1, dv)), resident(w_bhy), resident(w_bgla), resident(w_o),
        ],
        out_specs=pl.BlockSpec((tm, d), row),
        out_shape=jax.ShapeDtypeStruct((r, d), F32),
        compiler_params=_cparams("parallel"),
        name="mix_out",
    )(x2, mod_l, x0, z, rconv, o_f, o_b, p, p, p, skip, gn, w_bhy, w_bgla, w_o)


FF_CHUNK = 1024


def _mlp_body(x_ref, mod_ref, g_ref, w1_ref, w2_ref, fg_ref, o_ref, h_scr, *, d, final_norm):
    h_scr[...] = _modulated_norm(x_ref[...], g_ref[...], mod_ref[...], 3, 4, d).astype(BF16)
    dff = w1_ref.shape[1]
    acc = None
    for f0 in range(0, dff, FF_CHUNK):
        ff = slice(f0, min(f0 + FF_CHUNK, dff))
        a = jnp.maximum(jnp.dot(h_scr[...], w1_ref[:, ff], preferred_element_type=F32), 0.0)
        t = jnp.dot((a * a).astype(BF16), w2_ref[ff, :], preferred_element_type=F32)
        acc = t if acc is None else acc + t
    g2 = mod_ref[:, 5 * d:6 * d]
    y = x_ref[...] + g2 * acc
    if final_norm:
        y = y * lax.rsqrt(jnp.mean(y * y, axis=-1, keepdims=True) + EPS) * fg_ref[...]
    o_ref[...] = y


def _mlp(x2, mod_l, norm_g, layer, w1, w2, final_g, rows_per_cond, cond_base, final_norm):
    r, d = x2.shape
    tm = _tile(min(r, rows_per_cond), 512)
    bpc = rows_per_cond // tm
    resident = lambda a: pl.BlockSpec((None,) + a.shape[1:], lambda i: (layer, 0, 0), pipeline_mode=pl.Buffered(1))
    return pl.pallas_call(
        functools.partial(_mlp_body, d=d, final_norm=final_norm),
        grid=(r // tm,),
        in_specs=[
            pl.BlockSpec((tm, d), lambda i: (i, 0)),
            pl.BlockSpec((None, 1, N_MOD * d), lambda i: (cond_base + i // bpc, 0, 0)),
            pl.BlockSpec((1, d), lambda i: (0, 0)),
            resident(w1), resident(w2),
            pl.BlockSpec((1, d), lambda i: (0, 0)),
        ],
        out_specs=pl.BlockSpec((tm, d), lambda i: (i, 0)),
        out_shape=jax.ShapeDtypeStruct((r, d), F32),
        scratch_shapes=[pltpu.VMEM((tm, d), BF16)],
        compiler_params=_cparams("parallel"),
        name="mlp",
    )(x2, mod_l, norm_g, w1, w2, final_g)


def kernel(x, c, ctx, c_ctx, ada_w, ada_b, norm1_g, norm2_g, w_in, hy_conv_w, hy_conv_b, hy_filt_w1, hy_filt_b1, hy_filt_w2, hy_filt_b2, hy_filt_w3, hy_filt_freq, hy_decay, hy_skip, gla_gate_w, gla_gate_b, gla_norm_g, w_branch_hy, w_branch_gla, w_out, mlp_w1, mlp_w2, final_g):
    nb, seq, d = x.shape
    lctx = ctx.shape[1]
    depth = ada_w.shape[0]
    chy = hy_decay.shape[-1]
    rank = gla_gate_w.shape[2]
    dkt = gla_gate_w.shape[3]
    dvt = w_branch_gla.shape[1]
    dk, dv = dkt // GLA_HEADS, dvt // GLA_HEADS
    assert nb + 1 <= COND_ROWS and 2 * rank <= LANES

    cond = jnp.zeros((COND_ROWS, d), F32).at[:nb].set(c).at[nb].set(c_ctx)
    mod = _ada_mod(cond, ada_w, ada_b).reshape(depth, COND_ROWS, 1, N_MOD * d)

    sizes = (3 * chy, dkt, dkt, dvt, dvt, rank, rank, d, d)
    offs = np.concatenate([[0], np.cumsum(sizes)])
    w_hy = w_in[..., :offs[1]].astype(BF16)
    w_main = jnp.concatenate([w_in[..., offs[1]:offs[5]], w_in[..., offs[7]:]], axis=-1).astype(BF16)
    w_lr = jnp.pad(w_in[..., offs[5]:offs[7]], ((0, 0), (0, 0), (0, LANES - 2 * rank))).astype(BF16)
    conv_b = hy_conv_b.reshape(depth, 1, 3 * chy)
    q0 = 0
    k0 = q0 + dkt
    v0 = k0 + dkt
    rg0 = v0 + dvt
    gh0 = rg0 + dvt
    gg0 = gh0 + d

    gate_w_pad = jnp.zeros((depth, 2, LANES, dkt), F32)
    gate_w_pad = gate_w_pad.at[:, 0, :rank].set(gla_gate_w[:, 0]).at[:, 1, rank:2 * rank].set(gla_gate_w[:, 1])
    gate_w_pad = gate_w_pad.astype(BF16)

    w_bhy = w_branch_hy.astype(BF16)
    w_bgla = w_branch_gla.astype(BF16)
    w_o = w_out.astype(BF16)
    w1 = mlp_w1.astype(BF16)
    w2 = mlp_w2.astype(BF16)

    dft_lat, dft_ctx = _dft_constants(seq), _dft_constants(lctx)
    feat_lat, feat_ctx = _filter_features(seq), _filter_features(lctx)
    zero_state = jnp.zeros((nb, GLA_HEADS, dk, dv), F32)

    xs = x.reshape(nb * seq, d)
    cs = ctx.reshape(nb * lctx, d)
    for l in range(depth):
        last = l == depth - 1
        mod_l = mod[l]
        n1 = norm1_g[l].reshape(1, d)
        filt = (hy_filt_w1[l], hy_filt_b1[l], hy_filt_w2[l], hy_filt_b2[l], hy_filt_w3[l],
                hy_filt_freq[l], hy_decay[l])
        skip = hy_skip[l].reshape(1, chy)
        gn = gla_norm_g[l].reshape(1, dv)

        in_w = (l, w_hy, hy_conv_w, conv_b, w_main, w_lr)
        out_w = (l, w_bhy, w_bgla, w_o)
        x0_c, z_c, p_c, lr_c = _in_proj(cs, mod_l, n1, *in_w, nb * lctx, nb, lctx)
        of_c, ob_c, sf_c, sb_c = _gla_scan(p_c, lr_c, gate_w_pad[l], gla_gate_b[l], zero_state, zero_state,
                                           lctx, (q0, k0, v0))
        x0_l, z_l, p_l, lr_l = _in_proj(xs, mod_l, n1, *in_w, seq, 0, GRID_W)
        of_l, ob_l, _, _ = _gla_scan(p_l, lr_l, gate_w_pad[l], gla_gate_b[l], sf_c, sb_c, seq, (q0, k0, v0))

        h_l = _filter_spectrum(_hyena_filter_parts(feat_lat, *filt), dft_lat, seq)
        r_l = _long_conv(z_l, h_l, dft_lat, seq)
        xs = _mix_out(xs, mod_l, x0_l, z_l, r_l, of_l, ob_l, p_l, (rg0, gh0, gg0), skip, gn, *out_w, seq, 0)
        if not last:
            h_c = _filter_spectrum(_hyena_filter_parts(feat_ctx, *filt), dft_ctx, lctx)
            r_c = _long_conv(z_c, h_c, dft_ctx, lctx)
            cs = _mix_out(cs, mod_l, x0_c, z_c, r_c, of_c, ob_c, p_c, (rg0, gh0, gg0), skip, gn, *out_w,
                          nb * lctx, nb)

        n2 = norm2_g[l].reshape(1, d)
        fg = final_g.reshape(1, d)
        xs = _mlp(xs, mod_l, n2, l, w1, w2, fg, seq, 0, last)
        if not last:
            cs = _mlp(cs, mod_l, n2, l, w1, w2, fg, nb * lctx, nb, False)
    return xs.reshape(nb, seq, d)
```

```python
import functools
import math

import numpy as np
import jax
import jax.numpy as jnp
from jax import lax
from jax.experimental import pallas as pl
from jax.experimental.pallas import tpu as pltpu

F32 = jnp.float32
BF16 = jnp.bfloat16

GRID_W = 64
N_BANDS = 16
GLA_HEADS = 4
GLA_CHUNK = 64
GATE_TAU = 16.0
EPS = 1e-6
N_MOD = 6

V7X_VMEM_BYTES = 64 * 1024 * 1024
VMEM_LIMIT_BYTES = V7X_VMEM_BYTES * 3 // 4
LANES = 128
SUBLANES = 8
COND_ROWS = SUBLANES


def _cparams(*sem):
    return pltpu.CompilerParams(dimension_semantics=sem, vmem_limit_bytes=VMEM_LIMIT_BYTES)


def _tile(n, pref):
    t = min(n, pref)
    while n % t:
        t //= 2
    return t


def _sigmoid(x):
    return 1.0 / (1.0 + jnp.exp(-x))


def _silu(x):
    return x * _sigmoid(x)


def _log_sigmoid(x):
    return jnp.minimum(x, 0.0) - jnp.log(1.0 + jnp.exp(-jnp.abs(x)))


def _modulated_norm(x, g, mod, shift_idx, scale_idx, d):
    y = x * lax.rsqrt(jnp.mean(x * x, axis=-1, keepdims=True) + EPS) * g
    shift = mod[:, shift_idx * d:(shift_idx + 1) * d]
    scale = mod[:, scale_idx * d:(scale_idx + 1) * d]
    return y * (1.0 + scale) + shift


def _ada_body(c_ref, w_ref, b_ref, o_ref):
    s = _silu(c_ref[...]).astype(BF16)
    o_ref[...] = jnp.dot(s, w_ref[...].astype(BF16), preferred_element_type=F32) + b_ref[...]


def _ada_mod(cond, ada_w, ada_b):
    depth, d, n = ada_w.shape
    tn = _tile(n, 1536)
    return pl.pallas_call(
        _ada_body,
        grid=(depth, n // tn),
        in_specs=[
            pl.BlockSpec((COND_ROWS, d), lambda l, j: (0, 0)),
            pl.BlockSpec((None, d, tn), lambda l, j: (l, 0, j)),
            pl.BlockSpec((None, 1, tn), lambda l, j: (l, 0, j)),
        ],
        out_specs=pl.BlockSpec((None, COND_ROWS, tn), lambda l, j: (l, 0, j)),
        out_shape=jax.ShapeDtypeStruct((depth, COND_ROWS, n), F32),
        compiler_params=_cparams("parallel", "parallel"),
        name="ada_mod",
    )(cond, ada_w, ada_b.reshape(depth, 1, n))


MAIN_CHUNK = 1024
HYENA_CHUNK = 512


def _in_proj_body(x_ref, mod_ref, g_ref, why_ref, cw_ref, cb_ref, wm_ref, wlr_ref,
                  x0_ref, z_ref, lr_ref, *rest, d, c, period):
    seg_refs, h_scr = rest[:-1], rest[-1]
    h_scr[...] = _modulated_norm(x_ref[...], g_ref[...], mod_ref[...], 0, 1, d).astype(BF16)
    tm = x_ref.shape[0]
    pos = lax.broadcasted_iota(jnp.int32, (tm, 1), 0) % period
    first, last = pos == 0, pos == period - 1

    def conv(part, ch):
        cols = slice(part * c + ch.start, part * c + ch.stop)
        p = jnp.dot(h_scr[...], why_ref[:, cols], preferred_element_type=F32)
        prev = jnp.where(first, 0.0, pltpu.roll(p, 1, axis=0))
        nxt = jnp.where(last, 0.0, pltpu.roll(p, tm - 1, axis=0))
        return prev * cw_ref[0:1, cols] + p * cw_ref[1:2, cols] + nxt * cw_ref[2:3, cols] + cb_ref[:, cols]

    main_chunks, col = [], 0
    for ref in seg_refs:
        main_chunks.append((ref, slice(col, col + ref.shape[1])))
        col += ref.shape[1]
    hy_chunks = [slice(j0, min(j0 + HYENA_CHUNK, c)) for j0 in range(0, c, HYENA_CHUNK)]

    def main():
        if main_chunks:
            ref, cols = main_chunks.pop(0)
            ref[...] = jnp.dot(h_scr[...], wm_ref[:, cols], preferred_element_type=F32).astype(BF16)

    for ch in hy_chunks:
        x1 = conv(1, ch)
        main()
        z_ref[:, ch] = (x1 * conv(2, ch)).astype(BF16)
    for ch in hy_chunks:
        x0_ref[:, ch] = conv(0, ch).astype(BF16)
        main()
    while main_chunks:
        main()
    lr_ref[...] = jnp.dot(h_scr[...], wlr_ref[...], preferred_element_type=F32)


def _in_proj(x2, mod_l, norm_g, layer, w_hy, conv_w, conv_b, w_main, w_lr, seg_widths,
             rows_per_cond, cond_base, period):
    r, d = x2.shape
    c = w_hy.shape[2] // 3
    assert sum(seg_widths) == w_main.shape[2]
    tm = _tile(min(r, rows_per_cond), 512)
    assert tm % period == 0
    blocks_per_cond = rows_per_cond // tm
    resident = lambda a: pl.BlockSpec((None,) + a.shape[1:], lambda i: (layer,) + (0,) * (a.ndim - 1),
                                      pipeline_mode=pl.Buffered(1))
    rows = lambda w: pl.BlockSpec((tm, w), lambda i: (i, 0))
    return pl.pallas_call(
        functools.partial(_in_proj_body, d=d, c=c, period=period),
        grid=(r // tm,),
        in_specs=[
            rows(d),
            pl.BlockSpec((None, 1, N_MOD * d), lambda i: (cond_base + i // blocks_per_cond, 0, 0)),
            pl.BlockSpec((1, d), lambda i: (0, 0)),
            resident(w_hy), resident(conv_w), resident(conv_b), resident(w_main), resident(w_lr),
        ],
        out_specs=[rows(c), rows(c), rows(LANES)] + [rows(w) for w in seg_widths],
        out_shape=[jax.ShapeDtypeStruct((r, c), BF16), jax.ShapeDtypeStruct((r, c), BF16),
                   jax.ShapeDtypeStruct((r, LANES), F32)]
                  + [jax.ShapeDtypeStruct((r, w), BF16) for w in seg_widths],
        scratch_shapes=[pltpu.VMEM((tm, d), BF16)],
        compiler_params=_cparams("parallel"),
        name="in_proj",
    )(x2, mod_l, norm_g, w_hy, conv_w, conv_b, w_main, w_lr)


def _filter_body(feat_ref, w1_ref, b1_ref, w2_ref, b2_ref, w3_ref, fr_ref, dec_ref, fb_ref, *, c, tl):
    i = pl.program_id(0)

    def dot3(a, b):
        a_hi, a_lo = _split_bf16(a)
        b_hi, b_lo = _split_bf16(b)
        mm = lambda u, v: jnp.dot(u, v, preferred_element_type=F32)
        return mm(a_hi, b_hi) + mm(a_hi, b_lo) + mm(a_lo, b_hi)

    feat = feat_ref[...]
    fr = fr_ref[...]
    a = jnp.sin(fr * (dot3(feat, w1_ref[...]) + b1_ref[...]))
    a = jnp.sin(fr * (dot3(a, w2_ref[...]) + b2_ref[...]))
    hk = dot3(a, w3_ref[...])
    t = feat[:, 0:1]
    window = jnp.exp(-t * jnp.abs(dec_ref[...]))
    h_f = hk[:, :c] * window
    h_b = hk[:, c:] * window
    row = i * tl + lax.broadcasted_iota(jnp.int32, (tl, 1), 0)
    h_b = jnp.where(row == 0, 0.0, h_b)
    fb_ref[:, :c] = h_f.astype(BF16)
    fb_ref[:, c:] = h_b.astype(BF16)


def _pad_to(a, shape):
    return jnp.pad(a, [(0, s - n) for n, s in zip(a.shape, shape)])


def _hyena_filter_parts(feat, w1, b1, w2, b2, w3, freq, decay):
    l = feat.shape[0]
    c = decay.shape[-1]
    fe = fo = LANES
    assert feat.shape[1] <= fe and w1.shape[1] <= fo
    feat = _pad_to(feat, (l, fe))
    w1, w2, w3 = _pad_to(w1, (fe, fo)), _pad_to(w2, (fo, fo)), _pad_to(w3, (fo, 2 * c))
    b1, b2, freq = _pad_to(b1, (fo,)), _pad_to(b2, (fo,)), _pad_to(freq, (fo,))
    tl = _tile(l, 512)
    full = lambda shape: pl.BlockSpec(shape, lambda i: (0, 0))
    return pl.pallas_call(
        functools.partial(_filter_body, c=c, tl=tl),
        grid=(l // tl,),
        in_specs=[
            pl.BlockSpec((tl, fe), lambda i: (i, 0)),
            full((fe, fo)), full((1, fo)), full((fo, fo)), full((1, fo)), full((fo, 2 * c)),
            full((1, fo)), full((1, c)),
        ],
        out_specs=pl.BlockSpec((tl, 2 * c), lambda i: (i, 0)),
        out_shape=jax.ShapeDtypeStruct((l, 2 * c), BF16),
        compiler_params=_cparams("parallel"),
        name="hyena_filter",
    )(feat, w1, b1.reshape(1, fo), w2, b2.reshape(1, fo), w3, freq.reshape(1, fo), decay.reshape(1, c))


def _filter_features(l):
    t = jnp.linspace(0.0, 1.0, l, dtype=F32)[:, None]
    ang = (2.0 * math.pi / l) * jnp.arange(l, dtype=F32)[:, None] * \
        jnp.linspace(1e-4, N_BANDS - 1, N_BANDS, dtype=F32)[None, :]
    return jnp.concatenate([t, jnp.cos(ang), -jnp.sin(ang)], axis=-1)


DFT_BLOCK = 256
BF16_SUBLANES = 16


def _dft_plan(l):
    nb = min(DFT_BLOCK, l)
    nslab = l // nb
    grp = max(BF16_SUBLANES, LANES // nslab)
    assert l % nb == 0 and nb % grp == 0
    return nb, nslab, grp


def _dft_constants(l):
    nb, nslab, grp = _dft_plan(l)
    n = 2 * l
    nk1 = nslab + 1
    ngrp = nb // grp
    gi = np.arange(ngrp)[:, None, None]
    row = np.arange(nk1 * 2 * grp)[None, :, None]
    col = np.arange(nslab * grp)[None, None, :]
    k1, part, a_out = row // (2 * grp), (row // grp) % 2, row % grp
    n1, a_in = col // grp, col % grp
    t = nb * n1 + grp * gi + a_in
    ang = ((t * k1) % n) * (2.0 * math.pi / n)
    f1 = np.where(a_out == a_in, np.where(part == 0, np.cos(ang), -np.sin(ang)), 0.0)
    weight = np.where((k1 == 0) | (k1 == nslab), 1.0 / n, 2.0 / n)
    f1i = np.swapaxes(f1 * weight, 1, 2)
    r2 = np.arange(nb)
    th = ((r2[:, None] * r2[None, :]) % nb) * (2.0 * math.pi / nb)
    cs, sn = np.cos(th), np.sin(th)
    f2 = np.block([[cs, sn], [-sn, cs]])
    return tuple(jnp.asarray(m, F32).astype(BF16) for m in (f1, f1i, f2, f2.T))


def _unrolled_loop(n, width, fn):
    def body(i, carry):
        for j in range(width):
            fn(i * width + j)
        return carry

    if n >= width:
        lax.fori_loop(0, n // width, body, 0)
    for i in range(n - n % width, n):
        fn(i)


def _dft_stage1(srcs, f1_ref, t_scr, nb, nslab, grp):
    nk1 = nslab + 1
    lanes = t_scr.shape[-1]

    def group(gi):
        r0 = pl.multiple_of(gi * grp, grp)
        data = jnp.concatenate([s[:, pl.ds(r0, grp), :].reshape(nslab * grp, -1) for s in srcs], axis=1)
        t = jnp.dot(f1_ref[gi], data, preferred_element_type=F32)
        t_scr[:, :, pl.ds(r0, grp), :] = t.astype(BF16).reshape(nk1, 2, grp, lanes)

    _unrolled_loop(nb // grp, 4, group)


def _spectrum_body(hf_ref, hb_ref, f1_ref, f2_ref, h_ref, t_scr, *, nb, nslab, grp):
    ct = hf_ref.shape[-1]
    _dft_stage1([hf_ref, hb_ref], f1_ref, t_scr, nb, nslab, grp)

    def residue(k1):
        x = jnp.dot(f2_ref[...], t_scr[k1].reshape(2 * nb, 2 * ct), preferred_element_type=F32)
        h_ref[k1, :nb, :] = x[:nb, :ct] + x[:nb, ct:]
        h_ref[k1, nb:, :] = x[nb:, :ct] - x[nb:, ct:]

    _unrolled_loop(nslab + 1, 2, residue)


def _filter_spectrum(fb, consts, l):
    nb, nslab, grp = _dft_plan(l)
    f1, _, f2, _ = consts
    c = fb.shape[1] // 2
    ct = _tile(c, 256)
    ncj = c // ct
    nk1 = nslab + 1
    const = lambda a: pl.BlockSpec(a.shape, lambda j: (0,) * a.ndim, pipeline_mode=pl.Buffered(1))
    taps = lambda half: pl.BlockSpec((nslab, nb, ct), lambda j: (0, 0, half * ncj + j))
    fb3 = fb.reshape(nslab, nb, 2 * c)
    return pl.pallas_call(
        functools.partial(_spectrum_body, nb=nb, nslab=nslab, grp=grp),
        grid=(ncj,),
        in_specs=[taps(0), taps(1), const(f1), const(f2)],
        out_specs=pl.BlockSpec((nk1, 2 * nb, ct), lambda j: (0, 0, j)),
        out_shape=jax.ShapeDtypeStruct((nk1, 2 * nb, c), F32),
        scratch_shapes=[pltpu.VMEM((nk1, 2, nb, 2 * ct), BF16)],
        compiler_params=_cparams("parallel"),
        name="filter_spectrum",
    )(fb3, fb3, f1, f2)


def _long_conv_body(z_ref, h_ref, f1_ref, f1i_ref, f2_ref, f2i_ref, r_ref, t_scr, *, nb, nslab, grp):
    nseq, ct = z_ref.shape[0], z_ref.shape[-1]
    lanes = nseq * ct
    nk1 = nslab + 1
    _dft_stage1([z_ref.at[b] for b in range(nseq)], f1_ref, t_scr, nb, nslab, grp)

    def spectral(k1):
        x = jnp.dot(f2_ref[...], t_scr[k1].reshape(2 * nb, lanes), preferred_element_type=F32)
        xr, xi = x[:nb], x[nb:]
        hr = jnp.concatenate([h_ref[k1, :nb, :]] * nseq, axis=1)
        hi = jnp.concatenate([h_ref[k1, nb:, :]] * nseq, axis=1)
        y = jnp.concatenate([xr * hr - xi * hi, xr * hi + xi * hr], axis=0).astype(BF16)
        u = jnp.dot(f2i_ref[...], y, preferred_element_type=F32)
        t_scr[k1] = u.astype(BF16).reshape(2, nb, lanes)

    _unrolled_loop(nk1, 4, spectral)

    def inverse1(gi):
        r0 = pl.multiple_of(gi * grp, grp)
        data = t_scr[:, :, pl.ds(r0, grp), :].reshape(nk1 * 2 * grp, lanes)
        out = jnp.dot(f1i_ref[gi], data, preferred_element_type=F32).astype(r_ref.dtype)
        for b in range(nseq):
            r_ref[b, :, pl.ds(r0, grp), :] = out[:, b * ct:(b + 1) * ct].reshape(nslab, grp, ct)

    _unrolled_loop(nb // grp, 4, inverse1)


def _long_conv(zb, hspec, consts, l):
    nb, nslab, grp = _dft_plan(l)
    f1, f1i, f2, f2i = consts
    rows, c = zb.shape
    nbatch = rows // l
    nseq = 2 if nbatch % 2 == 0 else 1
    ct = _tile(c, 256)
    nk1 = nslab + 1
    single = pl.Buffered(1)
    const = lambda a: pl.BlockSpec(a.shape, lambda j, b: (0,) * a.ndim, pipeline_mode=single)
    seq_spec = pl.BlockSpec((nseq, nslab, nb, ct), lambda j, b: (b, 0, 0, j))
    out = pl.pallas_call(
        functools.partial(_long_conv_body, nb=nb, nslab=nslab, grp=grp),
        grid=(c // ct, nbatch // nseq),
        in_specs=[seq_spec, pl.BlockSpec((nk1, 2 * nb, ct), lambda j, b: (0, 0, j), pipeline_mode=single),
                  const(f1), const(f1i), const(f2), const(f2i)],
        out_specs=seq_spec,
        out_shape=jax.ShapeDtypeStruct((nbatch, nslab, nb, c), BF16),
        scratch_shapes=[pltpu.VMEM((nk1, 2, nb, nseq * ct), BF16)],
        compiler_params=_cparams("parallel", "parallel"),
        name="long_conv",
    )(zb.reshape(nbatch, nslab, nb, c), hspec, f1, f1i, f2, f2i)
    return out.reshape(rows, c)


def _split_bf16(x):
    hi = x.astype(BF16)
    lo = (x - hi.astype(F32)).astype(BF16)
    return hi, lo


def _gla_body(qf_ref, kf_ref, vf_ref, lrf_ref, qb_ref, kb_ref, vb_ref, lrb_ref, gw_ref, gb_ref,
              s0f_ref, s0b_ref, of_ref, ob_ref, sf_ref, sb_ref, *, dk, dv, gsz):
    step = pl.program_id(1)
    ck = GLA_CHUNK
    tb = gsz * ck

    @pl.when(step == 0)
    def _():
        sf_ref[...] = s0f_ref[...]
        sb_ref[...] = s0b_ref[...]

    rows = lax.broadcasted_iota(jnp.int32, (tb, tb), 0)
    cols = lax.broadcasted_iota(jnp.int32, (tb, tb), 1)
    same_chunk = (rows // ck) == (cols // ck)
    scale = dk ** -0.5

    def direction(q_ref, k_ref, v_ref, lr_ref, o_ref, s_ref, d):
        fwd = d == 0
        g = _log_sigmoid(jnp.dot(lr_ref[...].astype(BF16), gw_ref[d], preferred_element_type=F32)
                         + gb_ref[d]) / GATE_TAU
        keep = same_chunk & ((cols <= rows) if fwd else (cols >= rows))
        tri = keep.astype(BF16)
        g_hi, g_lo = _split_bf16(g)
        bcum = jnp.dot(tri, g_hi, preferred_element_type=F32) + jnp.dot(tri, g_lo, preferred_element_type=F32)
        ends = [bcum[(c * ck + ck - 1 if fwd else c * ck):(c * ck + ck if fwd else c * ck + 1), :] for c in range(gsz)]
        pad = [jnp.zeros_like(ends[0])] * (SUBLANES - gsz)
        decay = jnp.exp(jnp.concatenate(ends + pad, axis=0))
        decay_t = decay.T
        decay_rows = jnp.concatenate([jnp.broadcast_to(decay[c:c + 1], (ck, decay.shape[1])) for c in range(gsz)], axis=0)
        q = q_ref[...].astype(F32) * scale
        k_e = k_ref[...].astype(F32) * jnp.exp(-bcum)
        q_t = (q * jnp.exp(bcum)).astype(BF16)
        k_t = k_e.astype(BF16)
        k_d = (k_e * decay_rows).astype(BF16)
        order = range(gsz) if fwd else range(gsz - 1, -1, -1)
        for h in range(GLA_HEADS):
            kc = slice(h * dk, (h + 1) * dk)
            vc = slice(h * dv, (h + 1) * dv)
            v = v_ref[:, vc]
            att = lax.dot_general(q_t[:, kc], k_t[:, kc], (((1,), (1,)), ((), ())), preferred_element_type=F32)
            o_intra = jnp.dot(jnp.where(keep, att, 0.0).astype(BF16), v, preferred_element_type=F32)
            s = s_ref[h]
            for c in order:
                rs = slice(c * ck, (c + 1) * ck)
                o = o_intra[rs] + jnp.dot(q_t[rs, kc], s.astype(BF16), preferred_element_type=F32)
                o_ref[rs, vc] = o.astype(o_ref.dtype)
                upd = lax.dot_general(k_d[rs, kc], v[rs], (((0,), (0,)), ((), ())), preferred_element_type=F32)
                s = decay_t[kc, c:c + 1] * s + upd
            s_ref[h] = s

    for bi in range(qf_ref.shape[0]):
        at = lambda *refs: [r.at[bi] for r in refs]
        direction(*at(qf_ref, kf_ref, vf_ref, lrf_ref, of_ref, sf_ref), 0)
        direction(*at(qb_ref, kb_ref, vb_ref, lrb_ref, ob_ref, sb_ref), 1)


def _gla_scan(q, k, v, lr, gate_w, gate_b, s0_f, s0_b, seq_len):
    r = q.shape[0]
    nb = r // seq_len
    dkt = gate_w.shape[2]
    dk = dkt // GLA_HEADS
    dv = s0_f.shape[-1]
    dvt = dv * GLA_HEADS
    gsz = _tile(seq_len // GLA_CHUNK, 4)
    assert gsz <= SUBLANES
    tb = gsz * GLA_CHUNK
    n = seq_len // tb
    nseq = 2 if nb % 2 == 0 else 1
    fwd_blk = lambda i: i
    bwd_blk = lambda i: n - 1 - i

    def specs(blk):
        return [pl.BlockSpec((nseq, tb, w), lambda b, i: (b, blk(i), 0)) for w in (dkt, dkt, dvt, LANES)]

    state_spec = pl.BlockSpec((nseq, GLA_HEADS, dk, dv), lambda b, i: (b, 0, 0, 0))
    seqs = [a.reshape(nb, seq_len, a.shape[1]) for a in (q, k, v, lr)]
    o_f, o_b, s_f, s_b = pl.pallas_call(
        functools.partial(_gla_body, dk=dk, dv=dv, gsz=gsz),
        grid=(nb // nseq, n),
        in_specs=specs(fwd_blk) + specs(bwd_blk) + [
            pl.BlockSpec((2, LANES, dkt), lambda b, i: (0, 0, 0)),
            pl.BlockSpec((2, 1, dkt), lambda b, i: (0, 0, 0)),
            state_spec, state_spec,
        ],
        out_specs=[
            pl.BlockSpec((nseq, tb, dvt), lambda b, i: (b, fwd_blk(i), 0)),
            pl.BlockSpec((nseq, tb, dvt), lambda b, i: (b, bwd_blk(i), 0)),
            state_spec, state_spec,
        ],
        out_shape=[jax.ShapeDtypeStruct((nb, seq_len, dvt), BF16), jax.ShapeDtypeStruct((nb, seq_len, dvt), BF16),
                   jax.ShapeDtypeStruct(s0_f.shape, F32), jax.ShapeDtypeStruct(s0_b.shape, F32)],
        compiler_params=_cparams("parallel", "arbitrary"),
        name="gla_scan",
    )(*seqs, *seqs, gate_w, gate_b.reshape(2, 1, dkt), s0_f, s0_b)
    return o_f.reshape(r, dvt), o_b.reshape(r, dvt), s_f, s_b


def _mix_out_body(x_ref, mod_ref, x0_ref, z_ref, r_ref, of_ref, ob_ref, rg_ref, gh_ref, gg_ref,
                  skip_ref, gn_ref, wh_ref, wg_ref, wo_ref, o_ref, *, d, dv):
    f32 = lambda ref: ref[...].astype(F32)
    y_hy = f32(x0_ref) * (f32(r_ref) + skip_ref[...] * f32(z_ref))
    o = f32(of_ref) + f32(ob_ref)
    rg = f32(rg_ref)
    parts = []
    for h in range(GLA_HEADS):
        oh = o[:, h * dv:(h + 1) * dv]
        on = oh * lax.rsqrt(jnp.mean(oh * oh, axis=-1, keepdims=True) + EPS) * gn_ref[...]
        parts.append((on * _silu(rg[:, h * dv:(h + 1) * dv])).astype(BF16))
    y_gla = jnp.concatenate(parts, axis=-1)
    ph = jnp.dot(y_hy.astype(BF16), wh_ref[...], preferred_element_type=F32)
    pg = jnp.dot(y_gla, wg_ref[...], preferred_element_type=F32)
    merged = _sigmoid(f32(gh_ref)) * ph + _sigmoid(f32(gg_ref)) * pg
    out = jnp.dot(merged.astype(BF16), wo_ref[...], preferred_element_type=F32)
    g1 = mod_ref[:, 2 * d:3 * d]
    o_ref[...] = x_ref[...] + g1 * out


def _mix_out(x2, mod_l, x0, z, rconv, o_f, o_b, rg, gh, gg, skip, gn, layer, w_bhy, w_bgla, w_o,
             rows_per_cond, cond_base):
    r, d = x2.shape
    c = x0.shape[1]
    dvt = o_f.shape[1]
    dv = dvt // GLA_HEADS
    tm = _tile(min(r, rows_per_cond), 512)
    bpc = rows_per_cond // tm
    row = lambda i: (i, 0)
    full = lambda shape: pl.BlockSpec(shape, lambda i: (0, 0))
    resident = lambda a: pl.BlockSpec((None,) + a.shape[1:], lambda i: (layer, 0, 0), pipeline_mode=pl.Buffered(1))
    return pl.pallas_call(
        functools.partial(_mix_out_body, d=d, dv=dv),
        grid=(r // tm,),
        in_specs=[
            pl.BlockSpec((tm, d), row),
            pl.BlockSpec((None, 1, N_MOD * d), lambda i: (cond_base + i // bpc, 0, 0)),
            pl.BlockSpec((tm, c), row),
            pl.BlockSpec((tm, c), row),
            pl.BlockSpec((tm, c), row),
            pl.BlockSpec((tm, dvt), row),
            pl.BlockSpec((tm, dvt), row),
            pl.BlockSpec((tm, dvt), row),
            pl.BlockSpec((tm, d), row),
            pl.BlockSpec((tm, d), row),
            full((1, c)), full((1, dv)), resident(w_bhy), resident(w_bgla), resident(w_o),
        ],
        out_specs=pl.BlockSpec((tm, d), row),
        out_shape=jax.ShapeDtypeStruct((r, d), F32),
        compiler_params=_cparams("parallel"),
        name="mix_out",
    )(x2, mod_l, x0, z, rconv, o_f, o_b, rg, gh, gg, skip, gn, w_bhy, w_bgla, w_o)


FF_CHUNK = 1024


def _mlp_body(x_ref, mod_ref, g_ref, w1_ref, w2_ref, fg_ref, o_ref, h_scr, *, d, final_norm):
    h_scr[...] = _modulated_norm(x_ref[...], g_ref[...], mod_ref[...], 3, 4, d).astype(BF16)
    dff = w1_ref.shape[1]
    acc = None
    for f0 in range(0, dff, FF_CHUNK):
        ff = slice(f0, min(f0 + FF_CHUNK, dff))
        a = jnp.maximum(jnp.dot(h_scr[...], w1_ref[:, ff], preferred_element_type=F32), 0.0)
        t = jnp.dot((a * a).astype(BF16), w2_ref[ff, :], preferred_element_type=F32)
        acc = t if acc is None else acc + t
    g2 = mod_ref[:, 5 * d:6 * d]
    y = x_ref[...] + g2 * acc
    if final_norm:
        y = y * lax.rsqrt(jnp.mean(y * y, axis=-1, keepdims=True) + EPS) * fg_ref[...]
    o_ref[...] = y


def _mlp(x2, mod_l, norm_g, layer, w1, w2, final_g, rows_per_cond, cond_base, final_norm):
    r, d = x2.shape
    tm = _tile(min(r, rows_per_cond), 512)
    bpc = rows_per_cond // tm
    resident = lambda a: pl.BlockSpec((None,) + a.shape[1:], lambda i: (layer, 0, 0), pipeline_mode=pl.Buffered(1))
    return pl.pallas_call(
        functools.partial(_mlp_body, d=d, final_norm=final_norm),
        grid=(r // tm,),
        in_specs=[
            pl.BlockSpec((tm, d), lambda i: (i, 0)),
            pl.BlockSpec((None, 1, N_MOD * d), lambda i: (cond_base + i // bpc, 0, 0)),
            pl.BlockSpec((1, d), lambda i: (0, 0)),
            resident(w1), resident(w2),
            pl.BlockSpec((1, d), lambda i: (0, 0)),
        ],
        out_specs=pl.BlockSpec((tm, d), lambda i: (i, 0)),
        out_shape=jax.ShapeDtypeStruct((r, d), F32),
        scratch_shapes=[pltpu.VMEM((tm, d), BF16)],
        compiler_params=_cparams("parallel"),
        name="mlp",
    )(x2, mod_l, norm_g, w1, w2, final_g)


def kernel(x, c, ctx, c_ctx, ada_w, ada_b, norm1_g, norm2_g, w_in, hy_conv_w, hy_conv_b, hy_filt_w1, hy_filt_b1, hy_filt_w2, hy_filt_b2, hy_filt_w3, hy_filt_freq, hy_decay, hy_skip, gla_gate_w, gla_gate_b, gla_norm_g, w_branch_hy, w_branch_gla, w_out, mlp_w1, mlp_w2, final_g):
    nb, seq, d = x.shape
    lctx = ctx.shape[1]
    depth = ada_w.shape[0]
    chy = hy_decay.shape[-1]
    rank = gla_gate_w.shape[2]
    dkt = gla_gate_w.shape[3]
    dvt = w_branch_gla.shape[1]
    dk, dv = dkt // GLA_HEADS, dvt // GLA_HEADS
    assert nb + 1 <= COND_ROWS and 2 * rank <= LANES

    cond = jnp.zeros((COND_ROWS, d), F32).at[:nb].set(c).at[nb].set(c_ctx)
    mod = _ada_mod(cond, ada_w, ada_b).reshape(depth, COND_ROWS, 1, N_MOD * d)

    sizes = (3 * chy, dkt, dkt, dvt, dvt, rank, rank, d, d)
    offs = np.concatenate([[0], np.cumsum(sizes)])
    w_hy = w_in[..., :offs[1]].astype(BF16)
    w_main = jnp.concatenate([w_in[..., offs[1]:offs[5]], w_in[..., offs[7]:]], axis=-1).astype(BF16)
    w_lr = jnp.pad(w_in[..., offs[5]:offs[7]], ((0, 0), (0, 0), (0, LANES - 2 * rank))).astype(BF16)
    conv_b = hy_conv_b.reshape(depth, 1, 3 * chy)
    seg_widths = (dkt, dkt, dvt, dvt, d, d)

    gate_w_pad = jnp.zeros((depth, 2, LANES, dkt), F32)
    gate_w_pad = gate_w_pad.at[:, 0, :rank].set(gla_gate_w[:, 0]).at[:, 1, rank:2 * rank].set(gla_gate_w[:, 1])
    gate_w_pad = gate_w_pad.astype(BF16)

    w_bhy = w_branch_hy.astype(BF16)
    w_bgla = w_branch_gla.astype(BF16)
    w_o = w_out.astype(BF16)
    w1 = mlp_w1.astype(BF16)
    w2 = mlp_w2.astype(BF16)

    dft_lat, dft_ctx = _dft_constants(seq), _dft_constants(lctx)
    feat_lat, feat_ctx = _filter_features(seq), _filter_features(lctx)
    zero_state = jnp.zeros((nb, GLA_HEADS, dk, dv), F32)

    xs = x.reshape(nb * seq, d)
    cs = ctx.reshape(nb * lctx, d)
    for l in range(depth):
        last = l == depth - 1
        mod_l = mod[l]
        n1 = norm1_g[l].reshape(1, d)
        filt = (hy_filt_w1[l], hy_filt_b1[l], hy_filt_w2[l], hy_filt_b2[l], hy_filt_w3[l],
                hy_filt_freq[l], hy_decay[l])
        skip = hy_skip[l].reshape(1, chy)
        gn = gla_norm_g[l].reshape(1, dv)

        in_w = (l, w_hy, hy_conv_w, conv_b, w_main, w_lr, seg_widths)
        out_w = (l, w_bhy, w_bgla, w_o)
        gate = (gate_w_pad[l], gla_gate_b[l])
        x0_c, z_c, lr_c, q_c, k_c, v_c, *gates_c = _in_proj(cs, mod_l, n1, *in_w, nb * lctx, nb, lctx)
        of_c, ob_c, sf_c, sb_c = _gla_scan(q_c, k_c, v_c, lr_c, *gate, zero_state, zero_state, lctx)
        x0_l, z_l, lr_l, q_l, k_l, v_l, *gates_l = _in_proj(xs, mod_l, n1, *in_w, seq, 0, GRID_W)
        of_l, ob_l, _, _ = _gla_scan(q_l, k_l, v_l, lr_l, *gate, sf_c, sb_c, seq)

        h_l = _filter_spectrum(_hyena_filter_parts(feat_lat, *filt), dft_lat, seq)
        r_l = _long_conv(z_l, h_l, dft_lat, seq)
        xs = _mix_out(xs, mod_l, x0_l, z_l, r_l, of_l, ob_l, *gates_l, skip, gn, *out_w, seq, 0)
        if not last:
            h_c = _filter_spectrum(_hyena_filter_parts(feat_ctx, *filt), dft_ctx, lctx)
            r_c = _long_conv(z_c, h_c, dft_ctx, lctx)
            cs = _mix_out(cs, mod_l, x0_c, z_c, r_c, of_c, ob_c, *gates_c, skip, gn, *out_w, nb * lctx, nb)

        n2 = norm2_g[l].reshape(1, d)
        fg = final_g.reshape(1, d)
        xs = _mlp(xs, mod_l, n2, l, w1, w2, fg, seq, 0, last)
        if not last:
            cs = _mlp(cs, mod_l, n2, l, w1, w2, fg, nb * lctx, nb, False)
    return xs.reshape(nb, seq, d)
```

```python
import functools
import math

import numpy as np
import jax
import jax.numpy as jnp
from jax import lax
from jax.experimental import pallas as pl
from jax.experimental.pallas import tpu as pltpu

F32 = jnp.float32
BF16 = jnp.bfloat16

GRID_W = 64
N_BANDS = 16
GLA_HEADS = 4
GLA_CHUNK = 64
GATE_TAU = 16.0
EPS = 1e-6
N_MOD = 6

V7X_VMEM_BYTES = 64 * 1024 * 1024
VMEM_LIMIT_BYTES = V7X_VMEM_BYTES * 3 // 4
LANES = 128
SUBLANES = 8
COND_ROWS = SUBLANES


def _cparams(*sem):
    return pltpu.CompilerParams(dimension_semantics=sem, vmem_limit_bytes=VMEM_LIMIT_BYTES)


def _tile(n, pref):
    t = min(n, pref)
    while n % t:
        t //= 2
    return t


def _sigmoid(x):
    return 1.0 / (1.0 + jnp.exp(-x))


def _silu(x):
    return x * _sigmoid(x)


def _log_sigmoid(x):
    return jnp.minimum(x, 0.0) - jnp.log(1.0 + jnp.exp(-jnp.abs(x)))


def _split_bf16(x):
    hi = x.astype(BF16)
    lo = (x - hi.astype(F32)).astype(BF16)
    return hi, lo


def _modulated_norm(x, g, mod, shift_idx, scale_idx, d):
    y = x * lax.rsqrt(jnp.mean(x * x, axis=-1, keepdims=True) + EPS) * g
    shift = mod[:, shift_idx * d:(shift_idx + 1) * d]
    scale = mod[:, scale_idx * d:(scale_idx + 1) * d]
    return y * (1.0 + scale) + shift


def _ada_body(c_ref, w_ref, b_ref, o_ref):
    s = _silu(c_ref[...]).astype(BF16)
    o_ref[...] = jnp.dot(s, w_ref[...].astype(BF16), preferred_element_type=F32) + b_ref[...]


def _ada_mod(cond, ada_w, ada_b):
    depth, d, n = ada_w.shape
    tn = _tile(n, 1536)
    return pl.pallas_call(
        _ada_body,
        grid=(depth, n // tn),
        in_specs=[
            pl.BlockSpec((COND_ROWS, d), lambda l, j: (0, 0)),
            pl.BlockSpec((None, d, tn), lambda l, j: (l, 0, j)),
            pl.BlockSpec((None, 1, tn), lambda l, j: (l, 0, j)),
        ],
        out_specs=pl.BlockSpec((None, COND_ROWS, tn), lambda l, j: (l, 0, j)),
        out_shape=jax.ShapeDtypeStruct((depth, COND_ROWS, n), F32),
        compiler_params=_cparams("parallel", "parallel"),
        name="ada_mod",
    )(cond, ada_w, ada_b.reshape(depth, 1, n))


HYENA_CHUNK = 512


def _in_proj_body(x_ref, mod_ref, g_ref, why_ref, cw_ref, cb_ref, wm_ref, wlr_ref,
                  x0_ref, z_ref, lr_ref, *rest, d, c, period):
    seg_refs, h_scr = rest[:-1], rest[-1]
    h_scr[...] = _modulated_norm(x_ref[...], g_ref[...], mod_ref[...], 0, 1, d).astype(BF16)
    tm = x_ref.shape[0]
    pos = lax.broadcasted_iota(jnp.int32, (tm, 1), 0) % period
    first, last = pos == 0, pos == period - 1

    def conv(part, ch):
        cols = slice(part * c + ch.start, part * c + ch.stop)
        p = jnp.dot(h_scr[...], why_ref[:, cols], preferred_element_type=F32)
        prev = jnp.where(first, 0.0, pltpu.roll(p, 1, axis=0))
        nxt = jnp.where(last, 0.0, pltpu.roll(p, tm - 1, axis=0))
        return prev * cw_ref[0:1, cols] + p * cw_ref[1:2, cols] + nxt * cw_ref[2:3, cols] + cb_ref[:, cols]

    main_chunks, col = [], 0
    for ref in seg_refs:
        main_chunks.append((ref, slice(col, col + ref.shape[1])))
        col += ref.shape[1]
    hy_chunks = [slice(j0, min(j0 + HYENA_CHUNK, c)) for j0 in range(0, c, HYENA_CHUNK)]

    def main():
        if main_chunks:
            ref, cols = main_chunks.pop(0)
            ref[...] = jnp.dot(h_scr[...], wm_ref[:, cols], preferred_element_type=F32).astype(BF16)

    for ch in hy_chunks:
        x1 = conv(1, ch)
        main()
        z_ref[:, ch] = (x1 * conv(2, ch)).astype(BF16)
    for ch in hy_chunks:
        x0_ref[:, ch] = conv(0, ch).astype(BF16)
        main()
    while main_chunks:
        main()
    lr_ref[...] = jnp.dot(h_scr[...], wlr_ref[...], preferred_element_type=F32)


def _in_proj(x2, mod_l, norm_g, layer, w_hy, conv_w, conv_b, w_main, w_lr, seg_widths,
             rows_per_cond, cond_base, period):
    r, d = x2.shape
    c = w_hy.shape[2] // 3
    assert sum(seg_widths) == w_main.shape[2]
    tm = _tile(min(r, rows_per_cond), 512)
    assert tm % period == 0
    blocks_per_cond = rows_per_cond // tm
    resident = lambda a: pl.BlockSpec((None,) + a.shape[1:], lambda i: (layer,) + (0,) * (a.ndim - 1),
                                      pipeline_mode=pl.Buffered(1))
    rows = lambda w: pl.BlockSpec((tm, w), lambda i: (i, 0))
    return pl.pallas_call(
        functools.partial(_in_proj_body, d=d, c=c, period=period),
        grid=(r // tm,),
        in_specs=[
            rows(d),
            pl.BlockSpec((None, 1, N_MOD * d), lambda i: (cond_base + i // blocks_per_cond, 0, 0)),
            pl.BlockSpec((1, d), lambda i: (0, 0)),
            resident(w_hy), resident(conv_w), resident(conv_b), resident(w_main), resident(w_lr),
        ],
        out_specs=[rows(c), rows(c), rows(LANES)] + [rows(w) for w in seg_widths],
        out_shape=[jax.ShapeDtypeStruct((r, c), BF16), jax.ShapeDtypeStruct((r, c), BF16),
                   jax.ShapeDtypeStruct((r, LANES), F32)]
                  + [jax.ShapeDtypeStruct((r, w), BF16) for w in seg_widths],
        scratch_shapes=[pltpu.VMEM((tm, d), BF16)],
        compiler_params=_cparams("parallel"),
        name="in_proj",
    )(x2, mod_l, norm_g, w_hy, conv_w, conv_b, w_main, w_lr)


def _filter_body(feat_ref, w1_ref, b1_ref, w2_ref, b2_ref, w3_ref, fr_ref, dec_ref, fb_ref, *, c, tl):
    i = pl.program_id(0)

    def dot3(a, b):
        a_hi, a_lo = _split_bf16(a)
        b_hi, b_lo = _split_bf16(b)
        mm = lambda u, v: jnp.dot(u, v, preferred_element_type=F32)
        return mm(a_hi, b_hi) + mm(a_hi, b_lo) + mm(a_lo, b_hi)

    feat = feat_ref[...]
    fr = fr_ref[...]
    a = jnp.sin(fr * (dot3(feat, w1_ref[...]) + b1_ref[...]))
    a = jnp.sin(fr * (dot3(a, w2_ref[...]) + b2_ref[...]))
    hk = dot3(a, w3_ref[...])
    t = feat[:, 0:1]
    window = jnp.exp(-t * jnp.abs(dec_ref[...]))
    h_f = hk[:, :c] * window
    h_b = hk[:, c:] * window
    row = i * tl + lax.broadcasted_iota(jnp.int32, (tl, 1), 0)
    h_b = jnp.where(row == 0, 0.0, h_b)
    fb_ref[:, :c] = h_f.astype(BF16)
    fb_ref[:, c:] = h_b.astype(BF16)


def _pad_to(a, shape):
    return jnp.pad(a, [(0, s - n) for n, s in zip(a.shape, shape)])


def _hyena_filter_parts(feat, w1, b1, w2, b2, w3, freq, decay):
    l = feat.shape[0]
    c = decay.shape[-1]
    fe = fo = LANES
    assert feat.shape[1] <= fe and w1.shape[1] <= fo
    feat = _pad_to(feat, (l, fe))
    w1, w2, w3 = _pad_to(w1, (fe, fo)), _pad_to(w2, (fo, fo)), _pad_to(w3, (fo, 2 * c))
    b1, b2, freq = _pad_to(b1, (fo,)), _pad_to(b2, (fo,)), _pad_to(freq, (fo,))
    tl = _tile(l, 512)
    full = lambda shape: pl.BlockSpec(shape, lambda i: (0, 0))
    return pl.pallas_call(
        functools.partial(_filter_body, c=c, tl=tl),
        grid=(l // tl,),
        in_specs=[
            pl.BlockSpec((tl, fe), lambda i: (i, 0)),
            full((fe, fo)), full((1, fo)), full((fo, fo)), full((1, fo)), full((fo, 2 * c)),
            full((1, fo)), full((1, c)),
        ],
        out_specs=pl.BlockSpec((tl, 2 * c), lambda i: (i, 0)),
        out_shape=jax.ShapeDtypeStruct((l, 2 * c), BF16),
        compiler_params=_cparams("parallel"),
        name="hyena_filter",
    )(feat, w1, b1.reshape(1, fo), w2, b2.reshape(1, fo), w3, freq.reshape(1, fo), decay.reshape(1, c))


def _filter_features(l):
    t = jnp.linspace(0.0, 1.0, l, dtype=F32)[:, None]
    ang = (2.0 * math.pi / l) * jnp.arange(l, dtype=F32)[:, None] * \
        jnp.linspace(1e-4, N_BANDS - 1, N_BANDS, dtype=F32)[None, :]
    return jnp.concatenate([t, jnp.cos(ang), -jnp.sin(ang)], axis=-1)


DFT_BLOCK = 256
BF16_SUBLANES = 16


def _dft_plan(l):
    nb = min(DFT_BLOCK, l)
    nslab = l // nb
    grp = max(BF16_SUBLANES, LANES // nslab)
    assert l % nb == 0 and nb % grp == 0
    return nb, nslab, grp


def _dft_constants(l):
    nb, nslab, grp = _dft_plan(l)
    n = 2 * l
    nk1 = nslab + 1
    ngrp = nb // grp
    gi = np.arange(ngrp)[:, None, None]
    row = np.arange(nk1 * 2 * grp)[None, :, None]
    col = np.arange(nslab * grp)[None, None, :]
    k1, part, a_out = row // (2 * grp), (row // grp) % 2, row % grp
    n1, a_in = col // grp, col % grp
    t = nb * n1 + grp * gi + a_in
    ang = ((t * k1) % n) * (2.0 * math.pi / n)
    f1 = np.where(a_out == a_in, np.where(part == 0, np.cos(ang), -np.sin(ang)), 0.0)
    weight = np.where((k1 == 0) | (k1 == nslab), 1.0 / n, 2.0 / n)
    f1i = np.swapaxes(f1 * weight, 1, 2)
    r2 = np.arange(nb)
    th = ((r2[:, None] * r2[None, :]) % nb) * (2.0 * math.pi / nb)
    cs, sn = np.cos(th), np.sin(th)
    f2 = np.block([[cs, sn], [-sn, cs]])
    return tuple(jnp.asarray(m, F32).astype(BF16) for m in (f1, f1i, f2, f2.T))


def _unrolled_loop(n, width, fn):
    def body(i, carry):
        for j in range(width):
            fn(i * width + j)
        return carry

    if n >= width:
        lax.fori_loop(0, n // width, body, 0)
    for i in range(n - n % width, n):
        fn(i)


def _dft_stage1(srcs, f1_ref, t_scr, nb, nslab, grp):
    nk1 = nslab + 1
    lanes = t_scr.shape[-1]

    def group(gi):
        r0 = pl.multiple_of(gi * grp, grp)
        data = jnp.concatenate([s[:, pl.ds(r0, grp), :].reshape(nslab * grp, -1) for s in srcs], axis=1)
        t = jnp.dot(f1_ref[gi], data, preferred_element_type=F32)
        t_scr[:, :, pl.ds(r0, grp), :] = t.astype(BF16).reshape(nk1, 2, grp, lanes)

    _unrolled_loop(nb // grp, 4, group)


def _spectrum_body(hf_ref, hb_ref, f1_ref, f2_ref, h_ref, t_scr, *, nb, nslab, grp):
    ct = hf_ref.shape[-1]
    _dft_stage1([hf_ref, hb_ref], f1_ref, t_scr, nb, nslab, grp)

    def residue(k1):
        x = jnp.dot(f2_ref[...], t_scr[k1].reshape(2 * nb, 2 * ct), preferred_element_type=F32)
        h_ref[k1, :nb, :] = x[:nb, :ct] + x[:nb, ct:]
        h_ref[k1, nb:, :] = x[nb:, :ct] - x[nb:, ct:]

    _unrolled_loop(nslab + 1, 2, residue)


def _filter_spectrum(fb, consts, l):
    nb, nslab, grp = _dft_plan(l)
    f1, _, f2, _ = consts
    c = fb.shape[1] // 2
    ct = _tile(c, 256)
    ncj = c // ct
    nk1 = nslab + 1
    const = lambda a: pl.BlockSpec(a.shape, lambda j: (0,) * a.ndim, pipeline_mode=pl.Buffered(1))
    taps = lambda half: pl.BlockSpec((nslab, nb, ct), lambda j: (0, 0, half * ncj + j))
    fb3 = fb.reshape(nslab, nb, 2 * c)
    return pl.pallas_call(
        functools.partial(_spectrum_body, nb=nb, nslab=nslab, grp=grp),
        grid=(ncj,),
        in_specs=[taps(0), taps(1), const(f1), const(f2)],
        out_specs=pl.BlockSpec((nk1, 2 * nb, ct), lambda j: (0, 0, j)),
        out_shape=jax.ShapeDtypeStruct((nk1, 2 * nb, c), F32),
        scratch_shapes=[pltpu.VMEM((nk1, 2, nb, 2 * ct), BF16)],
        compiler_params=_cparams("parallel"),
        name="filter_spectrum",
    )(fb3, fb3, f1, f2)


def _long_conv_body(z_ref, h_ref, f1_ref, f1i_ref, f2_ref, f2i_ref, r_ref, t_scr, *, nb, nslab, grp):
    nseq, ct = z_ref.shape[0], z_ref.shape[-1]
    lanes = nseq * ct
    nk1 = nslab + 1
    _dft_stage1([z_ref.at[b] for b in range(nseq)], f1_ref, t_scr, nb, nslab, grp)

    def spectral(k1):
        x = jnp.dot(f2_ref[...], t_scr[k1].reshape(2 * nb, lanes), preferred_element_type=F32)
        xr, xi = x[:nb], x[nb:]
        hr = jnp.concatenate([h_ref[k1, :nb, :]] * nseq, axis=1)
        hi = jnp.concatenate([h_ref[k1, nb:, :]] * nseq, axis=1)
        y = jnp.concatenate([xr * hr - xi * hi, xr * hi + xi * hr], axis=0).astype(BF16)
        u = jnp.dot(f2i_ref[...], y, preferred_element_type=F32)
        t_scr[k1] = u.astype(BF16).reshape(2, nb, lanes)

    _unrolled_loop(nk1, 4, spectral)

    def inverse1(gi):
        r0 = pl.multiple_of(gi * grp, grp)
        data = t_scr[:, :, pl.ds(r0, grp), :].reshape(nk1 * 2 * grp, lanes)
        out = jnp.dot(f1i_ref[gi], data, preferred_element_type=F32).astype(r_ref.dtype)
        for b in range(nseq):
            r_ref[b, :, pl.ds(r0, grp), :] = out[:, b * ct:(b + 1) * ct].reshape(nslab, grp, ct)

    _unrolled_loop(nb // grp, 4, inverse1)


def _long_conv(zb, hspec, consts, l):
    nb, nslab, grp = _dft_plan(l)
    f1, f1i, f2, f2i = consts
    rows, c = zb.shape
    nbatch = rows // l
    nseq = 2 if nbatch % 2 == 0 else 1
    ct = _tile(c, 256)
    nk1 = nslab + 1
    single = pl.Buffered(1)
    const = lambda a: pl.BlockSpec(a.shape, lambda j, b: (0,) * a.ndim, pipeline_mode=single)
    seq_spec = pl.BlockSpec((nseq, nslab, nb, ct), lambda j, b: (b, 0, 0, j))
    out = pl.pallas_call(
        functools.partial(_long_conv_body, nb=nb, nslab=nslab, grp=grp),
        grid=(c // ct, nbatch // nseq),
        in_specs=[seq_spec, pl.BlockSpec((nk1, 2 * nb, ct), lambda j, b: (0, 0, j), pipeline_mode=single),
                  const(f1), const(f1i), const(f2), const(f2i)],
        out_specs=seq_spec,
        out_shape=jax.ShapeDtypeStruct((nbatch, nslab, nb, c), BF16),
        scratch_shapes=[pltpu.VMEM((nk1, 2, nb, nseq * ct), BF16)],
        compiler_params=_cparams("parallel", "parallel"),
        name="long_conv",
    )(zb.reshape(nbatch, nslab, nb, c), hspec, f1, f1i, f2, f2i)
    return out.reshape(rows, c)


def _gla_body(qf_ref, kf_ref, vf_ref, lrf_ref, qb_ref, kb_ref, vb_ref, lrb_ref, gw_ref, gb_ref,
              s0f_ref, s0b_ref, of_ref, ob_ref, sf_ref, sb_ref, *, dk, dv, gsz):
    step = pl.program_id(1)
    ck = GLA_CHUNK
    tb = gsz * ck

    @pl.when(step == 0)
    def _():
        sf_ref[...] = s0f_ref[...]
        sb_ref[...] = s0b_ref[...]

    rows = lax.broadcasted_iota(jnp.int32, (tb, tb), 0)
    cols = lax.broadcasted_iota(jnp.int32, (tb, tb), 1)
    same_chunk = (rows // ck) == (cols // ck)
    scale = dk ** -0.5

    def direction(q_ref, k_ref, v_ref, lr_ref, o_ref, s_ref, d):
        fwd = d == 0
        g = _log_sigmoid(jnp.dot(lr_ref[...].astype(BF16), gw_ref[d], preferred_element_type=F32)
                         + gb_ref[d]) / GATE_TAU
        keep = same_chunk & ((cols <= rows) if fwd else (cols >= rows))
        tri = keep.astype(BF16)
        g_hi, g_lo = _split_bf16(g)
        bcum = jnp.dot(tri, g_hi, preferred_element_type=F32) + jnp.dot(tri, g_lo, preferred_element_type=F32)
        ends = [bcum[(c * ck + ck - 1 if fwd else c * ck):(c * ck + ck if fwd else c * ck + 1), :] for c in range(gsz)]
        pad = [jnp.zeros_like(ends[0])] * (SUBLANES - gsz)
        decay = jnp.exp(jnp.concatenate(ends + pad, axis=0))
        decay_t = decay.T
        decay_rows = jnp.concatenate([jnp.broadcast_to(decay[c:c + 1], (ck, decay.shape[1])) for c in range(gsz)], axis=0)
        q = q_ref[...].astype(F32) * scale
        k_e = k_ref[...].astype(F32) * jnp.exp(-bcum)
        q_t = (q * jnp.exp(bcum)).astype(BF16)
        k_t = k_e.astype(BF16)
        k_d = (k_e * decay_rows).astype(BF16)
        order = range(gsz) if fwd else range(gsz - 1, -1, -1)
        for h in range(GLA_HEADS):
            kc = slice(h * dk, (h + 1) * dk)
            vc = slice(h * dv, (h + 1) * dv)
            v = v_ref[:, vc]
            att = lax.dot_general(q_t[:, kc], k_t[:, kc], (((1,), (1,)), ((), ())), preferred_element_type=F32)
            o_intra = jnp.dot(jnp.where(keep, att, 0.0).astype(BF16), v, preferred_element_type=F32)
            s = s_ref[h]
            for c in order:
                rs = slice(c * ck, (c + 1) * ck)
                o = o_intra[rs] + jnp.dot(q_t[rs, kc], s.astype(BF16), preferred_element_type=F32)
                o_ref[rs, vc] = o.astype(o_ref.dtype)
                upd = lax.dot_general(k_d[rs, kc], v[rs], (((0,), (0,)), ((), ())), preferred_element_type=F32)
                s = decay_t[kc, c:c + 1] * s + upd
            s_ref[h] = s

    for bi in range(qf_ref.shape[0]):
        at = lambda *refs: [r.at[bi] for r in refs]
        direction(*at(qf_ref, kf_ref, vf_ref, lrf_ref, of_ref, sf_ref), 0)
        direction(*at(qb_ref, kb_ref, vb_ref, lrb_ref, ob_ref, sb_ref), 1)


def _gla_scan(q, k, v, lr, gate_w, gate_b, s0_f, s0_b, seq_len):
    r = q.shape[0]
    nb = r // seq_len
    dkt = gate_w.shape[2]
    dk = dkt // GLA_HEADS
    dv = s0_f.shape[-1]
    dvt = dv * GLA_HEADS
    gsz = _tile(seq_len // GLA_CHUNK, 4)
    assert gsz <= SUBLANES
    tb = gsz * GLA_CHUNK
    n = seq_len // tb
    nseq = 2 if nb % 2 == 0 else 1
    fwd_blk = lambda i: i
    bwd_blk = lambda i: n - 1 - i

    def specs(blk):
        return [pl.BlockSpec((nseq, tb, w), lambda b, i: (b, blk(i), 0)) for w in (dkt, dkt, dvt, LANES)]

    state_spec = pl.BlockSpec((nseq, GLA_HEADS, dk, dv), lambda b, i: (b, 0, 0, 0))
    seqs = [a.reshape(nb, seq_len, a.shape[1]) for a in (q, k, v, lr)]
    o_f, o_b, s_f, s_b = pl.pallas_call(
        functools.partial(_gla_body, dk=dk, dv=dv, gsz=gsz),
        grid=(nb // nseq, n),
        in_specs=specs(fwd_blk) + specs(bwd_blk) + [
            pl.BlockSpec((2, LANES, dkt), lambda b, i: (0, 0, 0)),
            pl.BlockSpec((2, 1, dkt), lambda b, i: (0, 0, 0)),
            state_spec, state_spec,
        ],
        out_specs=[
            pl.BlockSpec((nseq, tb, dvt), lambda b, i: (b, fwd_blk(i), 0)),
            pl.BlockSpec((nseq, tb, dvt), lambda b, i: (b, bwd_blk(i), 0)),
            state_spec, state_spec,
        ],
        out_shape=[jax.ShapeDtypeStruct((nb, seq_len, dvt), BF16), jax.ShapeDtypeStruct((nb, seq_len, dvt), BF16),
                   jax.ShapeDtypeStruct(s0_f.shape, F32), jax.ShapeDtypeStruct(s0_b.shape, F32)],
        compiler_params=_cparams("parallel", "arbitrary"),
        name="gla_scan",
    )(*seqs, *seqs, gate_w, gate_b.reshape(2, 1, dkt), s0_f, s0_b)
    return o_f.reshape(r, dvt), o_b.reshape(r, dvt), s_f, s_b


def _mix_out_body(x_ref, mod_ref, x0_ref, z_ref, r_ref, of_ref, ob_ref, rg_ref, gh_ref, gg_ref,
                  skip_ref, gn_ref, wh_ref, wg_ref, wo_ref, o_ref, *, d, dv):
    f32 = lambda ref: ref[...].astype(F32)
    y_hy = f32(x0_ref) * (f32(r_ref) + skip_ref[...] * f32(z_ref))
    o = f32(of_ref) + f32(ob_ref)
    rg = f32(rg_ref)
    parts = []
    for h in range(GLA_HEADS):
        oh = o[:, h * dv:(h + 1) * dv]
        on = oh * lax.rsqrt(jnp.mean(oh * oh, axis=-1, keepdims=True) + EPS) * gn_ref[...]
        parts.append((on * _silu(rg[:, h * dv:(h + 1) * dv])).astype(BF16))
    y_gla = jnp.concatenate(parts, axis=-1)
    ph = jnp.dot(y_hy.astype(BF16), wh_ref[...], preferred_element_type=F32)
    pg = jnp.dot(y_gla, wg_ref[...], preferred_element_type=F32)
    merged = _sigmoid(f32(gh_ref)) * ph + _sigmoid(f32(gg_ref)) * pg
    out = jnp.dot(merged.astype(BF16), wo_ref[...], preferred_element_type=F32)
    g1 = mod_ref[:, 2 * d:3 * d]
    o_ref[...] = x_ref[...] + g1 * out


def _mix_out(x2, mod_l, x0, z, rconv, o_f, o_b, rg, gh, gg, skip, gn, layer, w_bhy, w_bgla, w_o,
             rows_per_cond, cond_base):
    r, d = x2.shape
    c = x0.shape[1]
    dvt = o_f.shape[1]
    dv = dvt // GLA_HEADS
    tm = _tile(min(r, rows_per_cond), 512)
    bpc = rows_per_cond // tm
    row = lambda i: (i, 0)
    full = lambda shape: pl.BlockSpec(shape, lambda i: (0, 0))
    resident = lambda a: pl.BlockSpec((None,) + a.shape[1:], lambda i: (layer, 0, 0), pipeline_mode=pl.Buffered(1))
    return pl.pallas_call(
        functools.partial(_mix_out_body, d=d, dv=dv),
        grid=(r // tm,),
        in_specs=[
            pl.BlockSpec((tm, d), row),
            pl.BlockSpec((None, 1, N_MOD * d), lambda i: (cond_base + i // bpc, 0, 0)),
            pl.BlockSpec((tm, c), row),
            pl.BlockSpec((tm, c), row),
            pl.BlockSpec((tm, c), row),
            pl.BlockSpec((tm, dvt), row),
            pl.BlockSpec((tm, dvt), row),
            pl.BlockSpec((tm, dvt), row),
            pl.BlockSpec((tm, d), row),
            pl.BlockSpec((tm, d), row),
            full((1, c)), full((1, dv)), resident(w_bhy), resident(w_bgla), resident(w_o),
        ],
        out_specs=pl.BlockSpec((tm, d), row),
        out_shape=jax.ShapeDtypeStruct((r, d), F32),
        compiler_params=_cparams("parallel"),
        name="mix_out",
    )(x2, mod_l, x0, z, rconv, o_f, o_b, rg, gh, gg, skip, gn, w_bhy, w_bgla, w_o)


FF_CHUNK = 1024


def _mlp_body(x_ref, mod_ref, g_ref, w1_ref, w2_ref, fg_ref, o_ref, h_scr, *, d, final_norm):
    h_scr[...] = _modulated_norm(x_ref[...], g_ref[...], mod_ref[...], 3, 4, d).astype(BF16)
    dff = w1_ref.shape[1]
    acc = None
    for f0 in range(0, dff, FF_CHUNK):
        ff = slice(f0, min(f0 + FF_CHUNK, dff))
        a = jnp.maximum(jnp.dot(h_scr[...], w1_ref[:, ff], preferred_element_type=F32), 0.0)
        t = jnp.dot((a * a).astype(BF16), w2_ref[ff, :], preferred_element_type=F32)
        acc = t if acc is None else acc + t
    g2 = mod_ref[:, 5 * d:6 * d]
    y = x_ref[...] + g2 * acc
    if final_norm:
        y = y * lax.rsqrt(jnp.mean(y * y, axis=-1, keepdims=True) + EPS) * fg_ref[...]
    o_ref[...] = y


def _mlp(x2, mod_l, norm_g, layer, w1, w2, final_g, rows_per_cond, cond_base, final_norm):
    r, d = x2.shape
    tm = _tile(min(r, rows_per_cond), 512)
    bpc = rows_per_cond // tm
    resident = lambda a: pl.BlockSpec((None,) + a.shape[1:], lambda i: (layer, 0, 0), pipeline_mode=pl.Buffered(1))
    return pl.pallas_call(
        functools.partial(_mlp_body, d=d, final_norm=final_norm),
        grid=(r // tm,),
        in_specs=[
            pl.BlockSpec((tm, d), lambda i: (i, 0)),
            pl.BlockSpec((None, 1, N_MOD * d), lambda i: (cond_base + i // bpc, 0, 0)),
            pl.BlockSpec((1, d), lambda i: (0, 0)),
            resident(w1), resident(w2),
            pl.BlockSpec((1, d), lambda i: (0, 0)),
        ],
        out_specs=pl.BlockSpec((tm, d), lambda i: (i, 0)),
        out_shape=jax.ShapeDtypeStruct((r, d), F32),
        scratch_shapes=[pltpu.VMEM((tm, d), BF16)],
        compiler_params=_cparams("parallel"),
        name="mlp",
    )(x2, mod_l, norm_g, w1, w2, final_g)


def kernel(x, c, ctx, c_ctx, ada_w, ada_b, norm1_g, norm2_g, w_in, hy_conv_w, hy_conv_b, hy_filt_w1, hy_filt_b1, hy_filt_w2, hy_filt_b2, hy_filt_w3, hy_filt_freq, hy_decay, hy_skip, gla_gate_w, gla_gate_b, gla_norm_g, w_branch_hy, w_branch_gla, w_out, mlp_w1, mlp_w2, final_g):
    nb, seq, d = x.shape
    lctx = ctx.shape[1]
    depth = ada_w.shape[0]
    chy = hy_decay.shape[-1]
    rank = gla_gate_w.shape[2]
    dkt = gla_gate_w.shape[3]
    dvt = w_branch_gla.shape[1]
    dk, dv = dkt // GLA_HEADS, dvt // GLA_HEADS
    assert nb + 1 <= COND_ROWS and 2 * rank <= LANES

    cond = jnp.zeros((COND_ROWS, d), F32).at[:nb].set(c).at[nb].set(c_ctx)
    mod = _ada_mod(cond, ada_w, ada_b).reshape(depth, COND_ROWS, 1, N_MOD * d)

    sizes = (3 * chy, dkt, dkt, dvt, dvt, rank, rank, d, d)
    offs = np.concatenate([[0], np.cumsum(sizes)])
    w_hy = w_in[..., :offs[1]].astype(BF16)
    w_main = jnp.concatenate([w_in[..., offs[1]:offs[5]], w_in[..., offs[7]:]], axis=-1).astype(BF16)
    w_lr = jnp.pad(w_in[..., offs[5]:offs[7]], ((0, 0), (0, 0), (0, LANES - 2 * rank))).astype(BF16)
    conv_b = hy_conv_b.reshape(depth, 1, 3 * chy)
    seg_widths = (dkt, dkt, dvt, dvt, d, d)

    gate_w_pad = jnp.zeros((depth, 2, LANES, dkt), F32)
    gate_w_pad = gate_w_pad.at[:, 0, :rank].set(gla_gate_w[:, 0]).at[:, 1, rank:2 * rank].set(gla_gate_w[:, 1])
    gate_w_pad = gate_w_pad.astype(BF16)

    w_bhy = w_branch_hy.astype(BF16)
    w_bgla = w_branch_gla.astype(BF16)
    w_o = w_out.astype(BF16)
    w1 = mlp_w1.astype(BF16)
    w2 = mlp_w2.astype(BF16)

    dft_lat, dft_ctx = _dft_constants(seq), _dft_constants(lctx)
    feat_lat, feat_ctx = _filter_features(seq), _filter_features(lctx)
    zero_state = jnp.zeros((nb, GLA_HEADS, dk, dv), F32)

    xs = x.reshape(nb * seq, d)
    cs = ctx.reshape(nb * lctx, d)
    for l in range(depth):
        last = l == depth - 1
        mod_l = mod[l]
        n1 = norm1_g[l].reshape(1, d)
        filt = (hy_filt_w1[l], hy_filt_b1[l], hy_filt_w2[l], hy_filt_b2[l], hy_filt_w3[l],
                hy_filt_freq[l], hy_decay[l])
        skip = hy_skip[l].reshape(1, chy)
        gn = gla_norm_g[l].reshape(1, dv)

        in_w = (l, w_hy, hy_conv_w, conv_b, w_main, w_lr, seg_widths)
        out_w = (l, w_bhy, w_bgla, w_o)
        gate = (gate_w_pad[l], gla_gate_b[l])
        x0_c, z_c, lr_c, q_c, k_c, v_c, *gates_c = _in_proj(cs, mod_l, n1, *in_w, nb * lctx, nb, lctx)
        of_c, ob_c, sf_c, sb_c = _gla_scan(q_c, k_c, v_c, lr_c, *gate, zero_state, zero_state, lctx)
        x0_l, z_l, lr_l, q_l, k_l, v_l, *gates_l = _in_proj(xs, mod_l, n1, *in_w, seq, 0, GRID_W)
        of_l, ob_l, _, _ = _gla_scan(q_l, k_l, v_l, lr_l, *gate, sf_c, sb_c, seq)

        h_l = _filter_spectrum(_hyena_filter_parts(feat_lat, *filt), dft_lat, seq)
        r_l = _long_conv(z_l, h_l, dft_lat, seq)
        xs = _mix_out(xs, mod_l, x0_l, z_l, r_l, of_l, ob_l, *gates_l, skip, gn, *out_w, seq, 0)
        if not last:
            h_c = _filter_spectrum(_hyena_filter_parts(feat_ctx, *filt), dft_ctx, lctx)
            r_c = _long_conv(z_c, h_c, dft_ctx, lctx)
            cs = _mix_out(cs, mod_l, x0_c, z_c, r_c, of_c, ob_c, *gates_c, skip, gn, *out_w, nb * lctx, nb)

        n2 = norm2_g[l].reshape(1, d)
        fg = final_g.reshape(1, d)
        xs = _mlp(xs, mod_l, n2, l, w1, w2, fg, seq, 0, last)
        if not last:
            cs = _mlp(cs, mod_l, n2, l, w1, w2, fg, nb * lctx, nb, False)
    return xs.reshape(nb, seq, d)
```

```python
import functools
import math

import numpy as np
import jax
import jax.numpy as jnp
from jax import lax
from jax.experimental import pallas as pl
from jax.experimental.pallas import tpu as pltpu

F32 = jnp.float32
BF16 = jnp.bfloat16

GRID_W = 64
N_BANDS = 16
GLA_HEADS = 4
GLA_CHUNK = 64
GATE_TAU = 16.0
EPS = 1e-6
N_MOD = 6

V7X_VMEM_BYTES = 64 * 1024 * 1024
VMEM_LIMIT_BYTES = V7X_VMEM_BYTES * 3 // 4
LANES = 128
SUBLANES = 8
COND_ROWS = SUBLANES


def _cparams(*sem, vmem_limit_bytes=VMEM_LIMIT_BYTES):
    return pltpu.CompilerParams(dimension_semantics=sem, vmem_limit_bytes=vmem_limit_bytes)


def _tile(n, pref):
    t = min(n, pref)
    while n % t:
        t //= 2
    return t


def _sigmoid(x):
    return 1.0 / (1.0 + jnp.exp(-x))


def _silu(x):
    return x * _sigmoid(x)


def _log_sigmoid(x):
    return jnp.minimum(x, 0.0) - jnp.log(1.0 + jnp.exp(-jnp.abs(x)))


def _split_bf16(x):
    hi = x.astype(BF16)
    lo = (x - hi.astype(F32)).astype(BF16)
    return hi, lo


def _modulated_norm(x, g, mod, shift_idx, scale_idx, d):
    y = x * lax.rsqrt(jnp.mean(x * x, axis=-1, keepdims=True) + EPS) * g
    shift = mod[:, shift_idx * d:(shift_idx + 1) * d]
    scale = mod[:, scale_idx * d:(scale_idx + 1) * d]
    return y * (1.0 + scale) + shift


def _ada_body(c_ref, w_ref, b_ref, o_ref):
    s = _silu(c_ref[...]).astype(BF16)
    o_ref[...] = jnp.dot(s, w_ref[...].astype(BF16), preferred_element_type=F32) + b_ref[...]


def _ada_mod(cond, ada_w, ada_b):
    depth, d, n = ada_w.shape
    tn = _tile(n, 1536)
    return pl.pallas_call(
        _ada_body,
        grid=(depth, n // tn),
        in_specs=[
            pl.BlockSpec((COND_ROWS, d), lambda l, j: (0, 0)),
            pl.BlockSpec((None, d, tn), lambda l, j: (l, 0, j)),
            pl.BlockSpec((None, 1, tn), lambda l, j: (l, 0, j)),
        ],
        out_specs=pl.BlockSpec((None, COND_ROWS, tn), lambda l, j: (l, 0, j)),
        out_shape=jax.ShapeDtypeStruct((depth, COND_ROWS, n), F32),
        compiler_params=_cparams("parallel", "parallel"),
        name="ada_mod",
    )(cond, ada_w, ada_b.reshape(depth, 1, n))


HYENA_CHUNK = 512


def _in_proj_body(x_ref, mod_ref, g_ref, why_ref, cw_ref, cb_ref, wm_ref, wlr_ref,
                  x0_ref, z_ref, lr_ref, *rest, d, c, period):
    seg_refs, h_scr = rest[:-1], rest[-1]
    h_scr[...] = _modulated_norm(x_ref[...], g_ref[...], mod_ref[...], 0, 1, d).astype(BF16)
    tm = x_ref.shape[0]
    pos = lax.broadcasted_iota(jnp.int32, (tm, 1), 0) % period
    first, last = pos == 0, pos == period - 1

    def conv(part, ch):
        cols = slice(part * c + ch.start, part * c + ch.stop)
        p = jnp.dot(h_scr[...], why_ref[:, cols], preferred_element_type=F32)
        prev = jnp.where(first, 0.0, pltpu.roll(p, 1, axis=0))
        nxt = jnp.where(last, 0.0, pltpu.roll(p, tm - 1, axis=0))
        return prev * cw_ref[0:1, cols] + p * cw_ref[1:2, cols] + nxt * cw_ref[2:3, cols] + cb_ref[:, cols]

    main_chunks, col = [], 0
    for ref in seg_refs:
        main_chunks.append((ref, slice(col, col + ref.shape[1])))
        col += ref.shape[1]
    hy_chunks = [slice(j0, min(j0 + HYENA_CHUNK, c)) for j0 in range(0, c, HYENA_CHUNK)]

    def main():
        if main_chunks:
            ref, cols = main_chunks.pop(0)
            ref[...] = jnp.dot(h_scr[...], wm_ref[:, cols], preferred_element_type=F32).astype(BF16)

    for ch in hy_chunks:
        x1 = conv(1, ch)
        main()
        z_ref[:, ch] = (x1 * conv(2, ch)).astype(BF16)
    for ch in hy_chunks:
        x0_ref[:, ch] = conv(0, ch).astype(BF16)
        main()
    while main_chunks:
        main()
    lr_ref[...] = jnp.dot(h_scr[...], wlr_ref[...], preferred_element_type=F32)


def _in_proj(x2, mod_l, norm_g, layer, w_hy, conv_w, conv_b, w_main, w_lr, seg_widths,
             rows_per_cond, cond_base, period):
    r, d = x2.shape
    c = w_hy.shape[2] // 3
    assert sum(seg_widths) == w_main.shape[2]
    tm = _tile(min(r, rows_per_cond), 512)
    assert tm % period == 0
    blocks_per_cond = rows_per_cond // tm
    resident = lambda a: pl.BlockSpec((None,) + a.shape[1:], lambda i: (layer,) + (0,) * (a.ndim - 1),
                                      pipeline_mode=pl.Buffered(1))
    rows = lambda w: pl.BlockSpec((tm, w), lambda i: (i, 0))
    return pl.pallas_call(
        functools.partial(_in_proj_body, d=d, c=c, period=period),
        grid=(r // tm,),
        in_specs=[
            rows(d),
            pl.BlockSpec((None, 1, N_MOD * d), lambda i: (cond_base + i // blocks_per_cond, 0, 0)),
            pl.BlockSpec((1, d), lambda i: (0, 0)),
            resident(w_hy), resident(conv_w), resident(conv_b), resident(w_main), resident(w_lr),
        ],
        out_specs=[rows(c), rows(c), rows(LANES)] + [rows(w) for w in seg_widths],
        out_shape=[jax.ShapeDtypeStruct((r, c), BF16), jax.ShapeDtypeStruct((r, c), BF16),
                   jax.ShapeDtypeStruct((r, LANES), F32)]
                  + [jax.ShapeDtypeStruct((r, w), BF16) for w in seg_widths],
        scratch_shapes=[pltpu.VMEM((tm, d), BF16)],
        compiler_params=_cparams("parallel"),
        name="in_proj",
    )(x2, mod_l, norm_g, w_hy, conv_w, conv_b, w_main, w_lr)


def _filter_body(feat_ref, w1_ref, b1_ref, w2_ref, b2_ref, w3_ref, fr_ref, dec_ref, fb_ref, *, c, tl):
    i = pl.program_id(0)

    def dot3(a, b):
        a_hi, a_lo = _split_bf16(a)
        b_hi, b_lo = _split_bf16(b)
        mm = lambda u, v: jnp.dot(u, v, preferred_element_type=F32)
        return mm(a_hi, b_hi) + mm(a_hi, b_lo) + mm(a_lo, b_hi)

    feat = feat_ref[...]
    fr = fr_ref[...]
    a = jnp.sin(fr * (dot3(feat, w1_ref[...]) + b1_ref[...]))
    a = jnp.sin(fr * (dot3(a, w2_ref[...]) + b2_ref[...]))
    hk = dot3(a, w3_ref[...])
    t = feat[:, 0:1]
    window = jnp.exp(-t * jnp.abs(dec_ref[...]))
    h_f = hk[:, :c] * window
    h_b = hk[:, c:] * window
    row = i * tl + lax.broadcasted_iota(jnp.int32, (tl, 1), 0)
    h_b = jnp.where(row == 0, 0.0, h_b)
    fb_ref[:, :c] = h_f.astype(BF16)
    fb_ref[:, c:] = h_b.astype(BF16)


def _pad_to(a, shape):
    return jnp.pad(a, [(0, s - n) for n, s in zip(a.shape, shape)])


def _hyena_filter_parts(feat, w1, b1, w2, b2, w3, freq, decay):
    l = feat.shape[0]
    c = decay.shape[-1]
    fe = fo = LANES
    assert feat.shape[1] <= fe and w1.shape[1] <= fo
    feat = _pad_to(feat, (l, fe))
    w1, w2, w3 = _pad_to(w1, (fe, fo)), _pad_to(w2, (fo, fo)), _pad_to(w3, (fo, 2 * c))
    b1, b2, freq = _pad_to(b1, (fo,)), _pad_to(b2, (fo,)), _pad_to(freq, (fo,))
    tl = _tile(l, 512)
    full = lambda shape: pl.BlockSpec(shape, lambda i: (0, 0))
    return pl.pallas_call(
        functools.partial(_filter_body, c=c, tl=tl),
        grid=(l // tl,),
        in_specs=[
            pl.BlockSpec((tl, fe), lambda i: (i, 0)),
            full((fe, fo)), full((1, fo)), full((fo, fo)), full((1, fo)), full((fo, 2 * c)),
            full((1, fo)), full((1, c)),
        ],
        out_specs=pl.BlockSpec((tl, 2 * c), lambda i: (i, 0)),
        out_shape=jax.ShapeDtypeStruct((l, 2 * c), BF16),
        compiler_params=_cparams("parallel"),
        name="hyena_filter",
    )(feat, w1, b1.reshape(1, fo), w2, b2.reshape(1, fo), w3, freq.reshape(1, fo), decay.reshape(1, c))


def _filter_features(l):
    t = jnp.linspace(0.0, 1.0, l, dtype=F32)[:, None]
    ang = (2.0 * math.pi / l) * jnp.arange(l, dtype=F32)[:, None] * \
        jnp.linspace(1e-4, N_BANDS - 1, N_BANDS, dtype=F32)[None, :]
    return jnp.concatenate([t, jnp.cos(ang), -jnp.sin(ang)], axis=-1)


DFT_BLOCK = 256
BF16_SUBLANES = 16


def _dft_plan(l):
    nb = min(DFT_BLOCK, l)
    nslab = l // nb
    grp = max(BF16_SUBLANES, LANES // nslab)
    assert l % nb == 0 and nb % grp == 0
    return nb, nslab, grp


def _dft_constants(l):
    nb, nslab, grp = _dft_plan(l)
    n = 2 * l
    nk1 = nslab + 1
    ngrp = nb // grp
    gi = np.arange(ngrp)[:, None, None]
    row = np.arange(nk1 * 2 * grp)[None, :, None]
    col = np.arange(nslab * grp)[None, None, :]
    k1, part, a_out = row // (2 * grp), (row // grp) % 2, row % grp
    n1, a_in = col // grp, col % grp
    t = nb * n1 + grp * gi + a_in
    ang = ((t * k1) % n) * (2.0 * math.pi / n)
    f1 = np.where(a_out == a_in, np.where(part == 0, np.cos(ang), -np.sin(ang)), 0.0)
    weight = np.where((k1 == 0) | (k1 == nslab), 1.0 / n, 2.0 / n)
    f1i = np.swapaxes(f1 * weight, 1, 2)
    r2 = np.arange(nb)
    th = ((r2[:, None] * r2[None, :]) % nb) * (2.0 * math.pi / nb)
    cs, sn = np.cos(th), np.sin(th)
    f2 = np.block([[cs, sn], [-sn, cs]])
    return tuple(jnp.asarray(m, F32).astype(BF16) for m in (f1, f1i, f2, f2.T))


def _unrolled_loop(n, width, fn):
    def body(i, carry):
        for j in range(width):
            fn(i * width + j)
        return carry

    if n >= width:
        lax.fori_loop(0, n // width, body, 0)
    for i in range(n - n % width, n):
        fn(i)


def _dft_stage1(srcs, f1_ref, t_scr, nb, nslab, grp):
    nk1 = nslab + 1
    lanes = t_scr.shape[-1]

    def group(gi):
        r0 = pl.multiple_of(gi * grp, grp)
        data = jnp.concatenate([s[:, pl.ds(r0, grp), :].reshape(nslab * grp, -1) for s in srcs], axis=1)
        t = jnp.dot(f1_ref[gi], data, preferred_element_type=F32)
        t_scr[:, :, pl.ds(r0, grp), :] = t.astype(BF16).reshape(nk1, 2, grp, lanes)

    _unrolled_loop(nb // grp, 4, group)


def _spectrum_body(hf_ref, hb_ref, f1_ref, f2_ref, h_ref, t_scr, *, nb, nslab, grp):
    ct = hf_ref.shape[-1]
    _dft_stage1([hf_ref, hb_ref], f1_ref, t_scr, nb, nslab, grp)

    def residue(k1):
        x = jnp.dot(f2_ref[...], t_scr[k1].reshape(2 * nb, 2 * ct), preferred_element_type=F32)
        h_ref[k1, :nb, :] = x[:nb, :ct] + x[:nb, ct:]
        h_ref[k1, nb:, :] = x[nb:, :ct] - x[nb:, ct:]

    _unrolled_loop(nslab + 1, 2, residue)


def _filter_spectrum(fb, consts, l):
    nb, nslab, grp = _dft_plan(l)
    f1, _, f2, _ = consts
    c = fb.shape[1] // 2
    ct = _tile(c, 256)
    ncj = c // ct
    nk1 = nslab + 1
    const = lambda a: pl.BlockSpec(a.shape, lambda j: (0,) * a.ndim, pipeline_mode=pl.Buffered(1))
    taps = lambda half: pl.BlockSpec((nslab, nb, ct), lambda j: (0, 0, half * ncj + j))
    fb3 = fb.reshape(nslab, nb, 2 * c)
    return pl.pallas_call(
        functools.partial(_spectrum_body, nb=nb, nslab=nslab, grp=grp),
        grid=(ncj,),
        in_specs=[taps(0), taps(1), const(f1), const(f2)],
        out_specs=pl.BlockSpec((nk1, 2 * nb, ct), lambda j: (0, 0, j)),
        out_shape=jax.ShapeDtypeStruct((nk1, 2 * nb, c), F32),
        scratch_shapes=[pltpu.VMEM((nk1, 2, nb, 2 * ct), BF16)],
        compiler_params=_cparams("parallel"),
        name="filter_spectrum",
    )(fb3, fb3, f1, f2)


def _long_conv_body(z_ref, h_ref, f1_ref, f1i_ref, f2_ref, f2i_ref, r_ref, t_scr, *, nb, nslab, grp):
    nseq, ct = z_ref.shape[0], z_ref.shape[-1]
    lanes = nseq * ct
    nk1 = nslab + 1
    _dft_stage1([z_ref.at[b] for b in range(nseq)], f1_ref, t_scr, nb, nslab, grp)

    def spectral(k1):
        x = jnp.dot(f2_ref[...], t_scr[k1].reshape(2 * nb, lanes), preferred_element_type=F32)
        xr, xi = x[:nb], x[nb:]
        hr = jnp.concatenate([h_ref[k1, :nb, :]] * nseq, axis=1)
        hi = jnp.concatenate([h_ref[k1, nb:, :]] * nseq, axis=1)
        y = jnp.concatenate([xr * hr - xi * hi, xr * hi + xi * hr], axis=0).astype(BF16)
        u = jnp.dot(f2i_ref[...], y, preferred_element_type=F32)
        t_scr[k1] = u.astype(BF16).reshape(2, nb, lanes)

    _unrolled_loop(nk1, 4, spectral)

    def inverse1(gi):
        r0 = pl.multiple_of(gi * grp, grp)
        data = t_scr[:, :, pl.ds(r0, grp), :].reshape(nk1 * 2 * grp, lanes)
        out = jnp.dot(f1i_ref[gi], data, preferred_element_type=F32).astype(r_ref.dtype)
        for b in range(nseq):
            r_ref[b, :, pl.ds(r0, grp), :] = out[:, b * ct:(b + 1) * ct].reshape(nslab, grp, ct)

    _unrolled_loop(nb // grp, 4, inverse1)


def _long_conv(zb, hspec, consts, l):
    nb, nslab, grp = _dft_plan(l)
    f1, f1i, f2, f2i = consts
    rows, c = zb.shape
    nbatch = rows // l
    nseq = 2 if nbatch % 2 == 0 else 1
    ct = _tile(c, 256)
    nk1 = nslab + 1
    single = pl.Buffered(1)
    const = lambda a: pl.BlockSpec(a.shape, lambda j, b: (0,) * a.ndim, pipeline_mode=single)
    seq_spec = pl.BlockSpec((nseq, nslab, nb, ct), lambda j, b: (b, 0, 0, j))
    out = pl.pallas_call(
        functools.partial(_long_conv_body, nb=nb, nslab=nslab, grp=grp),
        grid=(c // ct, nbatch // nseq),
        in_specs=[seq_spec, pl.BlockSpec((nk1, 2 * nb, ct), lambda j, b: (0, 0, j)),
                  const(f1), const(f1i), const(f2), const(f2i)],
        out_specs=seq_spec,
        out_shape=jax.ShapeDtypeStruct((nbatch, nslab, nb, c), BF16),
        scratch_shapes=[pltpu.VMEM((nk1, 2, nb, nseq * ct), BF16)],
        compiler_params=_cparams("parallel", "parallel", vmem_limit_bytes=V7X_VMEM_BYTES * 7 // 8),
        name="long_conv",
    )(zb.reshape(nbatch, nslab, nb, c), hspec, f1, f1i, f2, f2i)
    return out.reshape(rows, c)


def _gla_body(qf_ref, kf_ref, vf_ref, lrf_ref, qb_ref, kb_ref, vb_ref, lrb_ref, gw_ref, gb_ref,
              s0f_ref, s0b_ref, of_ref, ob_ref, sf_ref, sb_ref, *, dk, dv, gsz):
    step = pl.program_id(1)
    ck = GLA_CHUNK
    tb = gsz * ck

    @pl.when(step == 0)
    def _():
        sf_ref[...] = s0f_ref[...]
        sb_ref[...] = s0b_ref[...]

    rows = lax.broadcasted_iota(jnp.int32, (tb, tb), 0)
    cols = lax.broadcasted_iota(jnp.int32, (tb, tb), 1)
    same_chunk = (rows // ck) == (cols // ck)
    scale = dk ** -0.5

    def direction(q_ref, k_ref, v_ref, lr_ref, o_ref, s_ref, d):
        fwd = d == 0
        g = _log_sigmoid(jnp.dot(lr_ref[...].astype(BF16), gw_ref[d], preferred_element_type=F32)
                         + gb_ref[d]) / GATE_TAU
        keep = same_chunk & ((cols <= rows) if fwd else (cols >= rows))
        tri = keep.astype(BF16)
        g_hi, g_lo = _split_bf16(g)
        bcum = jnp.dot(tri, g_hi, preferred_element_type=F32) + jnp.dot(tri, g_lo, preferred_element_type=F32)
        ends = [bcum[(c * ck + ck - 1 if fwd else c * ck):(c * ck + ck if fwd else c * ck + 1), :] for c in range(gsz)]
        pad = [jnp.zeros_like(ends[0])] * (SUBLANES - gsz)
        decay = jnp.exp(jnp.concatenate(ends + pad, axis=0))
        decay_t = decay.T
        decay_rows = jnp.concatenate([jnp.broadcast_to(decay[c:c + 1], (ck, decay.shape[1])) for c in range(gsz)], axis=0)
        q = q_ref[...].astype(F32) * scale
        k_e = k_ref[...].astype(F32) * jnp.exp(-bcum)
        q_t = (q * jnp.exp(bcum)).astype(BF16)
        k_t = k_e.astype(BF16)
        k_d = (k_e * decay_rows).astype(BF16)
        order = range(gsz) if fwd else range(gsz - 1, -1, -1)
        for h in range(GLA_HEADS):
            kc = slice(h * dk, (h + 1) * dk)
            vc = slice(h * dv, (h + 1) * dv)
            v = v_ref[:, vc]
            att = lax.dot_general(q_t[:, kc], k_t[:, kc], (((1,), (1,)), ((), ())), preferred_element_type=F32)
            o_intra = jnp.dot(jnp.where(keep, att, 0.0).astype(BF16), v, preferred_element_type=F32)
            s = s_ref[h]
            for c in order:
                rs = slice(c * ck, (c + 1) * ck)
                o = o_intra[rs] + jnp.dot(q_t[rs, kc], s.astype(BF16), preferred_element_type=F32)
                o_ref[rs, vc] = o.astype(o_ref.dtype)
                upd = lax.dot_general(k_d[rs, kc], v[rs], (((0,), (0,)), ((), ())), preferred_element_type=F32)
                s = decay_t[kc, c:c + 1] * s + upd
            s_ref[h] = s

    for bi in range(qf_ref.shape[0]):
        at = lambda *refs: [r.at[bi] for r in refs]
        direction(*at(qf_ref, kf_ref, vf_ref, lrf_ref, of_ref, sf_ref), 0)
        direction(*at(qb_ref, kb_ref, vb_ref, lrb_ref, ob_ref, sb_ref), 1)


def _gla_scan(q, k, v, lr, gate_w, gate_b, s0_f, s0_b, seq_len):
    r = q.shape[0]
    nb = r // seq_len
    dkt = gate_w.shape[2]
    dk = dkt // GLA_HEADS
    dv = s0_f.shape[-1]
    dvt = dv * GLA_HEADS
    gsz = _tile(seq_len // GLA_CHUNK, 4)
    assert gsz <= SUBLANES
    tb = gsz * GLA_CHUNK
    n = seq_len // tb
    nseq = _tile(nb, 4)
    fwd_blk = lambda i: i
    bwd_blk = lambda i: n - 1 - i

    def specs(blk):
        return [pl.BlockSpec((nseq, tb, w), lambda b, i: (b, blk(i), 0)) for w in (dkt, dkt, dvt, LANES)]

    state_spec = pl.BlockSpec((nseq, GLA_HEADS, dk, dv), lambda b, i: (b, 0, 0, 0))
    seqs = [a.reshape(nb, seq_len, a.shape[1]) for a in (q, k, v, lr)]
    o_f, o_b, s_f, s_b = pl.pallas_call(
        functools.partial(_gla_body, dk=dk, dv=dv, gsz=gsz),
        grid=(nb // nseq, n),
        in_specs=specs(fwd_blk) + specs(bwd_blk) + [
            pl.BlockSpec((2, LANES, dkt), lambda b, i: (0, 0, 0)),
            pl.BlockSpec((2, 1, dkt), lambda b, i: (0, 0, 0)),
            state_spec, state_spec,
        ],
        out_specs=[
            pl.BlockSpec((nseq, tb, dvt), lambda b, i: (b, fwd_blk(i), 0)),
            pl.BlockSpec((nseq, tb, dvt), lambda b, i: (b, bwd_blk(i), 0)),
            state_spec, state_spec,
        ],
        out_shape=[jax.ShapeDtypeStruct((nb, seq_len, dvt), BF16), jax.ShapeDtypeStruct((nb, seq_len, dvt), BF16),
                   jax.ShapeDtypeStruct(s0_f.shape, F32), jax.ShapeDtypeStruct(s0_b.shape, F32)],
        compiler_params=_cparams("parallel", "arbitrary"),
        name="gla_scan",
    )(*seqs, *seqs, gate_w, gate_b.reshape(2, 1, dkt), s0_f, s0_b)
    return o_f.reshape(r, dvt), o_b.reshape(r, dvt), s_f, s_b


def _mix_out_body(x_ref, mod_ref, x0_ref, z_ref, r_ref, of_ref, ob_ref, rg_ref, gh_ref, gg_ref,
                  skip_ref, gn_ref, wh_ref, wg_ref, wo_ref, o_ref, *, d, dv):
    f32 = lambda ref: ref[...].astype(F32)
    y_hy = f32(x0_ref) * (f32(r_ref) + skip_ref[...] * f32(z_ref))
    o = f32(of_ref) + f32(ob_ref)
    rg = f32(rg_ref)
    parts = []
    for h in range(GLA_HEADS):
        oh = o[:, h * dv:(h + 1) * dv]
        on = oh * lax.rsqrt(jnp.mean(oh * oh, axis=-1, keepdims=True) + EPS) * gn_ref[...]
        parts.append((on * _silu(rg[:, h * dv:(h + 1) * dv])).astype(BF16))
    y_gla = jnp.concatenate(parts, axis=-1)
    ph = jnp.dot(y_hy.astype(BF16), wh_ref[...], preferred_element_type=F32)
    pg = jnp.dot(y_gla, wg_ref[...], preferred_element_type=F32)
    merged = _sigmoid(f32(gh_ref)) * ph + _sigmoid(f32(gg_ref)) * pg
    out = jnp.dot(merged.astype(BF16), wo_ref[...], preferred_element_type=F32)
    g1 = mod_ref[:, 2 * d:3 * d]
    o_ref[...] = x_ref[...] + g1 * out


def _mix_out(x2, mod_l, x0, z, rconv, o_f, o_b, rg, gh, gg, skip, gn, layer, w_bhy, w_bgla, w_o,
             rows_per_cond, cond_base):
    r, d = x2.shape
    c = x0.shape[1]
    dvt = o_f.shape[1]
    dv = dvt // GLA_HEADS
    tm = _tile(min(r, rows_per_cond), 512)
    bpc = rows_per_cond // tm
    row = lambda i: (i, 0)
    full = lambda shape: pl.BlockSpec(shape, lambda i: (0, 0))
    resident = lambda a: pl.BlockSpec((None,) + a.shape[1:], lambda i: (layer, 0, 0), pipeline_mode=pl.Buffered(1))
    return pl.pallas_call(
        functools.partial(_mix_out_body, d=d, dv=dv),
        grid=(r // tm,),
        in_specs=[
            pl.BlockSpec((tm, d), row),
            pl.BlockSpec((None, 1, N_MOD * d), lambda i: (cond_base + i // bpc, 0, 0)),
            pl.BlockSpec((tm, c), row),
            pl.BlockSpec((tm, c), row),
            pl.BlockSpec((tm, c), row),
            pl.BlockSpec((tm, dvt), row),
            pl.BlockSpec((tm, dvt), row),
            pl.BlockSpec((tm, dvt), row),
            pl.BlockSpec((tm, d), row),
            pl.BlockSpec((tm, d), row),
            full((1, c)), full((1, dv)), resident(w_bhy), resident(w_bgla), resident(w_o),
        ],
        out_specs=pl.BlockSpec((tm, d), row),
        out_shape=jax.ShapeDtypeStruct((r, d), F32),
        compiler_params=_cparams("parallel"),
        name="mix_out",
    )(x2, mod_l, x0, z, rconv, o_f, o_b, rg, gh, gg, skip, gn, w_bhy, w_bgla, w_o)


FF_CHUNK = 1024


def _mlp_body(x_ref, mod_ref, g_ref, w1_ref, w2_ref, fg_ref, o_ref, h_scr, *, d, final_norm):
    h_scr[...] = _modulated_norm(x_ref[...], g_ref[...], mod_ref[...], 3, 4, d).astype(BF16)
    dff = w1_ref.shape[1]
    acc = None
    for f0 in range(0, dff, FF_CHUNK):
        ff = slice(f0, min(f0 + FF_CHUNK, dff))
        a = jnp.maximum(jnp.dot(h_scr[...], w1_ref[:, ff], preferred_element_type=F32), 0.0)
        t = jnp.dot((a * a).astype(BF16), w2_ref[ff, :], preferred_element_type=F32)
        acc = t if acc is None else acc + t
    g2 = mod_ref[:, 5 * d:6 * d]
    y = x_ref[...] + g2 * acc
    if final_norm:
        y = y * lax.rsqrt(jnp.mean(y * y, axis=-1, keepdims=True) + EPS) * fg_ref[...]
    o_ref[...] = y


def _mlp(x2, mod_l, norm_g, layer, w1, w2, final_g, rows_per_cond, cond_base, final_norm):
    r, d = x2.shape
    tm = _tile(min(r, rows_per_cond), 512)
    bpc = rows_per_cond // tm
    resident = lambda a: pl.BlockSpec((None,) + a.shape[1:], lambda i: (layer, 0, 0), pipeline_mode=pl.Buffered(1))
    return pl.pallas_call(
        functools.partial(_mlp_body, d=d, final_norm=final_norm),
        grid=(r // tm,),
        in_specs=[
            pl.BlockSpec((tm, d), lambda i: (i, 0)),
            pl.BlockSpec((None, 1, N_MOD * d), lambda i: (cond_base + i // bpc, 0, 0)),
            pl.BlockSpec((1, d), lambda i: (0, 0)),
            resident(w1), resident(w2),
            pl.BlockSpec((1, d), lambda i: (0, 0)),
        ],
        out_specs=pl.BlockSpec((tm, d), lambda i: (i, 0)),
        out_shape=jax.ShapeDtypeStruct((r, d), F32),
        scratch_shapes=[pltpu.VMEM((tm, d), BF16)],
        compiler_params=_cparams("parallel"),
        name="mlp",
    )(x2, mod_l, norm_g, w1, w2, final_g)


def kernel(x, c, ctx, c_ctx, ada_w, ada_b, norm1_g, norm2_g, w_in, hy_conv_w, hy_conv_b, hy_filt_w1, hy_filt_b1, hy_filt_w2, hy_filt_b2, hy_filt_w3, hy_filt_freq, hy_decay, hy_skip, gla_gate_w, gla_gate_b, gla_norm_g, w_branch_hy, w_branch_gla, w_out, mlp_w1, mlp_w2, final_g):
    nb, seq, d = x.shape
    lctx = ctx.shape[1]
    depth = ada_w.shape[0]
    chy = hy_decay.shape[-1]
    rank = gla_gate_w.shape[2]
    dkt = gla_gate_w.shape[3]
    dvt = w_branch_gla.shape[1]
    dk, dv = dkt // GLA_HEADS, dvt // GLA_HEADS
    assert nb + 1 <= COND_ROWS and 2 * rank <= LANES

    cond = jnp.zeros((COND_ROWS, d), F32).at[:nb].set(c).at[nb].set(c_ctx)
    mod = _ada_mod(cond, ada_w, ada_b).reshape(depth, COND_ROWS, 1, N_MOD * d)

    sizes = (3 * chy, dkt, dkt, dvt, dvt, rank, rank, d, d)
    offs = np.concatenate([[0], np.cumsum(sizes)])
    w_hy = w_in[..., :offs[1]].astype(BF16)
    w_main = jnp.concatenate([w_in[..., offs[1]:offs[5]], w_in[..., offs[7]:]], axis=-1).astype(BF16)
    w_lr = jnp.pad(w_in[..., offs[5]:offs[7]], ((0, 0), (0, 0), (0, LANES - 2 * rank))).astype(BF16)
    conv_b = hy_conv_b.reshape(depth, 1, 3 * chy)
    seg_widths = (dkt, dkt, dvt, dvt, d, d)

    gate_w_pad = jnp.zeros((depth, 2, LANES, dkt), F32)
    gate_w_pad = gate_w_pad.at[:, 0, :rank].set(gla_gate_w[:, 0]).at[:, 1, rank:2 * rank].set(gla_gate_w[:, 1])
    gate_w_pad = gate_w_pad.astype(BF16)

    w_bhy = w_branch_hy.astype(BF16)
    w_bgla = w_branch_gla.astype(BF16)
    w_o = w_out.astype(BF16)
    w1 = mlp_w1.astype(BF16)
    w2 = mlp_w2.astype(BF16)

    dft_lat, dft_ctx = _dft_constants(seq), _dft_constants(lctx)
    feat_lat, feat_ctx = _filter_features(seq), _filter_features(lctx)
    zero_state = jnp.zeros((nb, GLA_HEADS, dk, dv), F32)

    xs = x.reshape(nb * seq, d)
    cs = ctx.reshape(nb * lctx, d)
    for l in range(depth):
        last = l == depth - 1
        mod_l = mod[l]
        n1 = norm1_g[l].reshape(1, d)
        filt = (hy_filt_w1[l], hy_filt_b1[l], hy_filt_w2[l], hy_filt_b2[l], hy_filt_w3[l],
                hy_filt_freq[l], hy_decay[l])
        skip = hy_skip[l].reshape(1, chy)
        gn = gla_norm_g[l].reshape(1, dv)

        in_w = (l, w_hy, hy_conv_w, conv_b, w_main, w_lr, seg_widths)
        out_w = (l, w_bhy, w_bgla, w_o)
        gate = (gate_w_pad[l], gla_gate_b[l])
        x0_c, z_c, lr_c, q_c, k_c, v_c, *gates_c = _in_proj(cs, mod_l, n1, *in_w, nb * lctx, nb, lctx)
        of_c, ob_c, sf_c, sb_c = _gla_scan(q_c, k_c, v_c, lr_c, *gate, zero_state, zero_state, lctx)
        x0_l, z_l, lr_l, q_l, k_l, v_l, *gates_l = _in_proj(xs, mod_l, n1, *in_w, seq, 0, GRID_W)
        of_l, ob_l, _, _ = _gla_scan(q_l, k_l, v_l, lr_l, *gate, sf_c, sb_c, seq)

        h_l = _filter_spectrum(_hyena_filter_parts(feat_lat, *filt), dft_lat, seq)
        r_l = _long_conv(z_l, h_l, dft_lat, seq)
        xs = _mix_out(xs, mod_l, x0_l, z_l, r_l, of_l, ob_l, *gates_l, skip, gn, *out_w, seq, 0)
        if not last:
            h_c = _filter_spectrum(_hyena_filter_parts(feat_ctx, *filt), dft_ctx, lctx)
            r_c = _long_conv(z_c, h_c, dft_ctx, lctx)
            cs = _mix_out(cs, mod_l, x0_c, z_c, r_c, of_c, ob_c, *gates_c, skip, gn, *out_w, nb * lctx, nb)

        n2 = norm2_g[l].reshape(1, d)
        fg = final_g.reshape(1, d)
        xs = _mlp(xs, mod_l, n2, l, w1, w2, fg, seq, 0, last)
        if not last:
            cs = _mlp(cs, mod_l, n2, l, w1, w2, fg, nb * lctx, nb, False)
    return xs.reshape(nb, seq, d)
```

```python
import functools
import math

import numpy as np
import jax
import jax.numpy as jnp
from jax import lax
from jax.experimental import pallas as pl
from jax.experimental.pallas import tpu as pltpu

F32 = jnp.float32
BF16 = jnp.bfloat16

GRID_W = 64
N_BANDS = 16
GLA_HEADS = 4
GLA_CHUNK = 64
GATE_TAU = 16.0
EPS = 1e-6
N_MOD = 6

V7X_VMEM_BYTES = 64 * 1024 * 1024
VMEM_LIMIT_BYTES = V7X_VMEM_BYTES * 3 // 4
LANES = 128
SUBLANES = 8
COND_ROWS = SUBLANES


def _cparams(*sem, vmem_limit_bytes=VMEM_LIMIT_BYTES):
    return pltpu.CompilerParams(dimension_semantics=sem, vmem_limit_bytes=vmem_limit_bytes)


def _tile(n, pref):
    t = min(n, pref)
    while n % t:
        t //= 2
    return t


def _sigmoid(x):
    return 1.0 / (1.0 + jnp.exp(-x))


def _silu(x):
    return x * _sigmoid(x)


def _log_sigmoid(x):
    return jnp.minimum(x, 0.0) - jnp.log(1.0 + jnp.exp(-jnp.abs(x)))


def _split_bf16(x):
    hi = x.astype(BF16)
    lo = (x - hi.astype(F32)).astype(BF16)
    return hi, lo


def _modulated_norm(x, g, mod, shift_idx, scale_idx, d):
    y = x * lax.rsqrt(jnp.mean(x * x, axis=-1, keepdims=True) + EPS) * g
    shift = mod[:, shift_idx * d:(shift_idx + 1) * d]
    scale = mod[:, scale_idx * d:(scale_idx + 1) * d]
    return y * (1.0 + scale) + shift


def _ada_body(c_ref, w_ref, b_ref, o_ref):
    s = _silu(c_ref[...]).astype(BF16)
    o_ref[...] = jnp.dot(s, w_ref[...].astype(BF16), preferred_element_type=F32) + b_ref[...]


def _ada_mod(cond, ada_w, ada_b):
    depth, d, n = ada_w.shape
    tn = _tile(n, 1536)
    return pl.pallas_call(
        _ada_body,
        grid=(depth, n // tn),
        in_specs=[
            pl.BlockSpec((COND_ROWS, d), lambda l, j: (0, 0)),
            pl.BlockSpec((None, d, tn), lambda l, j: (l, 0, j)),
            pl.BlockSpec((None, 1, tn), lambda l, j: (l, 0, j)),
        ],
        out_specs=pl.BlockSpec((None, COND_ROWS, tn), lambda l, j: (l, 0, j)),
        out_shape=jax.ShapeDtypeStruct((depth, COND_ROWS, n), F32),
        compiler_params=_cparams("parallel", "parallel"),
        name="ada_mod",
    )(cond, ada_w, ada_b.reshape(depth, 1, n))


HYENA_CHUNK = 512


def _in_proj_body(x_ref, mod_ref, g_ref, why_ref, cw_ref, cb_ref, wm_ref, wlr_ref,
                  x0_ref, z_ref, lr_ref, *rest, d, c, period):
    seg_refs, h_scr = rest[:-1], rest[-1]
    h_scr[...] = _modulated_norm(x_ref[...], g_ref[...], mod_ref[...], 0, 1, d).astype(BF16)
    tm = x_ref.shape[0]
    pos = lax.broadcasted_iota(jnp.int32, (tm, 1), 0) % period
    first, last = pos == 0, pos == period - 1

    def conv(part, ch):
        cols = slice(part * c + ch.start, part * c + ch.stop)
        p = jnp.dot(h_scr[...], why_ref[:, cols], preferred_element_type=F32)
        prev = jnp.where(first, 0.0, pltpu.roll(p, 1, axis=0))
        nxt = jnp.where(last, 0.0, pltpu.roll(p, tm - 1, axis=0))
        return prev * cw_ref[0:1, cols] + p * cw_ref[1:2, cols] + nxt * cw_ref[2:3, cols] + cb_ref[:, cols]

    main_chunks, col = [], 0
    for ref in seg_refs:
        main_chunks.append((ref, slice(col, col + ref.shape[1])))
        col += ref.shape[1]
    hy_chunks = [slice(j0, min(j0 + HYENA_CHUNK, c)) for j0 in range(0, c, HYENA_CHUNK)]

    def main():
        if main_chunks:
            ref, cols = main_chunks.pop(0)
            ref[...] = jnp.dot(h_scr[...], wm_ref[:, cols], preferred_element_type=F32).astype(BF16)

    for ch in hy_chunks:
        x1 = conv(1, ch)
        main()
        z_ref[:, ch] = (x1 * conv(2, ch)).astype(BF16)
    for ch in hy_chunks:
        x0_ref[:, ch] = conv(0, ch).astype(BF16)
        main()
    while main_chunks:
        main()
    lr_ref[...] = jnp.dot(h_scr[...], wlr_ref[...], preferred_element_type=F32)


def _in_proj(x2, mod_l, norm_g, layer, w_hy, conv_w, conv_b, w_main, w_lr, seg_widths,
             rows_per_cond, cond_base, period):
    r, d = x2.shape
    c = w_hy.shape[2] // 3
    assert sum(seg_widths) == w_main.shape[2]
    tm = _tile(min(r, rows_per_cond), 512)
    assert tm % period == 0
    blocks_per_cond = rows_per_cond // tm
    resident = lambda a: pl.BlockSpec((None,) + a.shape[1:], lambda i: (layer,) + (0,) * (a.ndim - 1),
                                      pipeline_mode=pl.Buffered(1))
    rows = lambda w: pl.BlockSpec((tm, w), lambda i: (i, 0))
    return pl.pallas_call(
        functools.partial(_in_proj_body, d=d, c=c, period=period),
        grid=(r // tm,),
        in_specs=[
            rows(d),
            pl.BlockSpec((None, 1, N_MOD * d), lambda i: (cond_base + i // blocks_per_cond, 0, 0)),
            pl.BlockSpec((1, d), lambda i: (0, 0)),
            resident(w_hy), resident(conv_w), resident(conv_b), resident(w_main), resident(w_lr),
        ],
        out_specs=[rows(c), rows(c), rows(LANES)] + [rows(w) for w in seg_widths],
        out_shape=[jax.ShapeDtypeStruct((r, c), BF16), jax.ShapeDtypeStruct((r, c), BF16),
                   jax.ShapeDtypeStruct((r, LANES), F32)]
                  + [jax.ShapeDtypeStruct((r, w), BF16) for w in seg_widths],
        scratch_shapes=[pltpu.VMEM((tm, d), BF16)],
        compiler_params=_cparams("parallel"),
        name="in_proj",
    )(x2, mod_l, norm_g, w_hy, conv_w, conv_b, w_main, w_lr)


def _filter_body(feat_ref, w1_ref, b1_ref, w2_ref, b2_ref, w3_ref, fr_ref, dec_ref, fb_ref, *, c, tl):
    i = pl.program_id(0)

    def dot3(a, b):
        a_hi, a_lo = _split_bf16(a)
        b_hi, b_lo = _split_bf16(b)
        mm = lambda u, v: jnp.dot(u, v, preferred_element_type=F32)
        return mm(a_hi, b_hi) + mm(a_hi, b_lo) + mm(a_lo, b_hi)

    feat = feat_ref[...]
    fr = fr_ref[...]
    a = jnp.sin(fr * (dot3(feat, w1_ref[...]) + b1_ref[...]))
    a = jnp.sin(fr * (dot3(a, w2_ref[...]) + b2_ref[...]))
    hk = dot3(a, w3_ref[...])
    t = feat[:, 0:1]
    window = jnp.exp(-t * jnp.abs(dec_ref[...]))
    h_f = hk[:, :c] * window
    h_b = hk[:, c:] * window
    row = i * tl + lax.broadcasted_iota(jnp.int32, (tl, 1), 0)
    h_b = jnp.where(row == 0, 0.0, h_b)
    fb_ref[:, :c] = h_f.astype(BF16)
    fb_ref[:, c:] = h_b.astype(BF16)


def _pad_to(a, shape):
    return jnp.pad(a, [(0, s - n) for n, s in zip(a.shape, shape)])


def _hyena_filter_parts(feat, w1, b1, w2, b2, w3, freq, decay):
    l = feat.shape[0]
    c = decay.shape[-1]
    fe = fo = LANES
    assert feat.shape[1] <= fe and w1.shape[1] <= fo
    feat = _pad_to(feat, (l, fe))
    w1, w2, w3 = _pad_to(w1, (fe, fo)), _pad_to(w2, (fo, fo)), _pad_to(w3, (fo, 2 * c))
    b1, b2, freq = _pad_to(b1, (fo,)), _pad_to(b2, (fo,)), _pad_to(freq, (fo,))
    tl = _tile(l, 512)
    full = lambda shape: pl.BlockSpec(shape, lambda i: (0, 0))
    return pl.pallas_call(
        functools.partial(_filter_body, c=c, tl=tl),
        grid=(l // tl,),
        in_specs=[
            pl.BlockSpec((tl, fe), lambda i: (i, 0)),
            full((fe, fo)), full((1, fo)), full((fo, fo)), full((1, fo)), full((fo, 2 * c)),
            full((1, fo)), full((1, c)),
        ],
        out_specs=pl.BlockSpec((tl, 2 * c), lambda i: (i, 0)),
        out_shape=jax.ShapeDtypeStruct((l, 2 * c), BF16),
        compiler_params=_cparams("parallel"),
        name="hyena_filter",
    )(feat, w1, b1.reshape(1, fo), w2, b2.reshape(1, fo), w3, freq.reshape(1, fo), decay.reshape(1, c))


def _filter_features(l):
    t = jnp.linspace(0.0, 1.0, l, dtype=F32)[:, None]
    ang = (2.0 * math.pi / l) * jnp.arange(l, dtype=F32)[:, None] * \
        jnp.linspace(1e-4, N_BANDS - 1, N_BANDS, dtype=F32)[None, :]
    return jnp.concatenate([t, jnp.cos(ang), -jnp.sin(ang)], axis=-1)


DFT_BLOCK = 256
BF16_SUBLANES = 16


def _dft_plan(l):
    nb = min(DFT_BLOCK, l)
    nslab = l // nb
    grp = max(BF16_SUBLANES, LANES // nslab)
    assert l % nb == 0 and nb % grp == 0
    return nb, nslab, grp


def _dft_constants(l):
    nb, nslab, grp = _dft_plan(l)
    n = 2 * l
    nk1 = nslab + 1
    ngrp = nb // grp
    gi = np.arange(ngrp)[:, None, None]
    row = np.arange(nk1 * 2 * grp)[None, :, None]
    col = np.arange(nslab * grp)[None, None, :]
    k1, part, a_out = row // (2 * grp), (row // grp) % 2, row % grp
    n1, a_in = col // grp, col % grp
    t = nb * n1 + grp * gi + a_in
    ang = ((t * k1) % n) * (2.0 * math.pi / n)
    f1 = np.where(a_out == a_in, np.where(part == 0, np.cos(ang), -np.sin(ang)), 0.0)
    weight = np.where((k1 == 0) | (k1 == nslab), 1.0 / n, 2.0 / n)
    f1i = np.swapaxes(f1 * weight, 1, 2)
    r2 = np.arange(nb)
    th = ((r2[:, None] * r2[None, :]) % nb) * (2.0 * math.pi / nb)
    cs, sn = np.cos(th), np.sin(th)
    f2 = np.block([[cs, sn], [-sn, cs]])
    return tuple(jnp.asarray(m, F32).astype(BF16) for m in (f1, f1i, f2, f2.T))


def _unrolled_loop(n, width, fn):
    def body(i, carry):
        for j in range(width):
            fn(i * width + j)
        return carry

    if n >= width:
        lax.fori_loop(0, n // width, body, 0)
    for i in range(n - n % width, n):
        fn(i)


def _dft_stage1(srcs, f1_ref, t_scr, nb, nslab, grp):
    nk1 = nslab + 1
    lanes = t_scr.shape[-1]

    def group(gi):
        r0 = pl.multiple_of(gi * grp, grp)
        data = jnp.concatenate([s[:, pl.ds(r0, grp), :].reshape(nslab * grp, -1) for s in srcs], axis=1)
        t = jnp.dot(f1_ref[gi], data, preferred_element_type=F32)
        t_scr[:, :, pl.ds(r0, grp), :] = t.astype(BF16).reshape(nk1, 2, grp, lanes)

    _unrolled_loop(nb // grp, 4, group)


def _spectrum_body(hf_ref, hb_ref, f1_ref, f2_ref, h_ref, t_scr, *, nb, nslab, grp):
    ct = hf_ref.shape[-1]
    _dft_stage1([hf_ref, hb_ref], f1_ref, t_scr, nb, nslab, grp)

    def residue(k1):
        x = jnp.dot(f2_ref[...], t_scr[k1].reshape(2 * nb, 2 * ct), preferred_element_type=F32)
        h_ref[k1, :nb, :] = x[:nb, :ct] + x[:nb, ct:]
        h_ref[k1, nb:, :] = x[nb:, :ct] - x[nb:, ct:]

    _unrolled_loop(nslab + 1, 2, residue)


def _filter_spectrum(fb, consts, l):
    nb, nslab, grp = _dft_plan(l)
    f1, _, f2, _ = consts
    c = fb.shape[1] // 2
    ct = _tile(c, 256)
    ncj = c // ct
    nk1 = nslab + 1
    const = lambda a: pl.BlockSpec(a.shape, lambda j: (0,) * a.ndim, pipeline_mode=pl.Buffered(1))
    taps = lambda half: pl.BlockSpec((nslab, nb, ct), lambda j: (0, 0, half * ncj + j))
    fb3 = fb.reshape(nslab, nb, 2 * c)
    return pl.pallas_call(
        functools.partial(_spectrum_body, nb=nb, nslab=nslab, grp=grp),
        grid=(ncj,),
        in_specs=[taps(0), taps(1), const(f1), const(f2)],
        out_specs=pl.BlockSpec((nk1, 2 * nb, ct), lambda j: (0, 0, j)),
        out_shape=jax.ShapeDtypeStruct((nk1, 2 * nb, c), F32),
        scratch_shapes=[pltpu.VMEM((nk1, 2, nb, 2 * ct), BF16)],
        compiler_params=_cparams("parallel"),
        name="filter_spectrum",
    )(fb3, fb3, f1, f2)


def _long_conv_body(z_ref, h_ref, f1_ref, f1i_ref, f2_ref, f2i_ref, r_ref, t_scr, *, nb, nslab, grp):
    nseq, ct = z_ref.shape[0], z_ref.shape[-1]
    lanes = nseq * ct
    nk1 = nslab + 1
    _dft_stage1([z_ref.at[b] for b in range(nseq)], f1_ref, t_scr, nb, nslab, grp)

    def spectral(k1):
        x = jnp.dot(f2_ref[...], t_scr[k1].reshape(2 * nb, lanes), preferred_element_type=F32)
        xr, xi = x[:nb], x[nb:]
        hr = jnp.concatenate([h_ref[k1, :nb, :]] * nseq, axis=1)
        hi = jnp.concatenate([h_ref[k1, nb:, :]] * nseq, axis=1)
        y = jnp.concatenate([xr * hr - xi * hi, xr * hi + xi * hr], axis=0).astype(BF16)
        u = jnp.dot(f2i_ref[...], y, preferred_element_type=F32)
        t_scr[k1] = u.astype(BF16).reshape(2, nb, lanes)

    _unrolled_loop(nk1, 4, spectral)

    def inverse1(gi):
        r0 = pl.multiple_of(gi * grp, grp)
        data = t_scr[:, :, pl.ds(r0, grp), :].reshape(nk1 * 2 * grp, lanes)
        out = jnp.dot(f1i_ref[gi], data, preferred_element_type=F32).astype(r_ref.dtype)
        for b in range(nseq):
            r_ref[b, :, pl.ds(r0, grp), :] = out[:, b * ct:(b + 1) * ct].reshape(nslab, grp, ct)

    _unrolled_loop(nb // grp, 4, inverse1)


def _long_conv(zb, hspec, consts, l):
    nb, nslab, grp = _dft_plan(l)
    f1, f1i, f2, f2i = consts
    rows, c = zb.shape
    nbatch = rows // l
    nseq = 2 if nbatch % 2 == 0 else 1
    ct = _tile(c, 256)
    nk1 = nslab + 1
    single = pl.Buffered(1)
    const = lambda a: pl.BlockSpec(a.shape, lambda j, b: (0,) * a.ndim, pipeline_mode=single)
    seq_spec = pl.BlockSpec((nseq, nslab, nb, ct), lambda j, b: (b, 0, 0, j))
    out = pl.pallas_call(
        functools.partial(_long_conv_body, nb=nb, nslab=nslab, grp=grp),
        grid=(c // ct, nbatch // nseq),
        in_specs=[seq_spec, pl.BlockSpec((nk1, 2 * nb, ct), lambda j, b: (0, 0, j)),
                  const(f1), const(f1i), const(f2), const(f2i)],
        out_specs=seq_spec,
        out_shape=jax.ShapeDtypeStruct((nbatch, nslab, nb, c), BF16),
        scratch_shapes=[pltpu.VMEM((nk1, 2, nb, nseq * ct), BF16)],
        compiler_params=_cparams("parallel", "parallel", vmem_limit_bytes=V7X_VMEM_BYTES * 7 // 8),
        name="long_conv",
    )(zb.reshape(nbatch, nslab, nb, c), hspec, f1, f1i, f2, f2i)
    return out.reshape(rows, c)


def _gla_body(qf_ref, kf_ref, vf_ref, lrf_ref, qb_ref, kb_ref, vb_ref, lrb_ref, gw_ref, gb_ref,
              s0f_ref, s0b_ref, of_ref, ob_ref, sf_ref, sb_ref, *, dk, dv, gsz):
    step = pl.program_id(1)
    ck = GLA_CHUNK
    tb = gsz * ck

    @pl.when(step == 0)
    def _():
        sf_ref[...] = s0f_ref[...]
        sb_ref[...] = s0b_ref[...]

    rows = lax.broadcasted_iota(jnp.int32, (tb, tb), 0)
    cols = lax.broadcasted_iota(jnp.int32, (tb, tb), 1)
    same_chunk = (rows // ck) == (cols // ck)
    scale = dk ** -0.5

    def direction(q_ref, k_ref, v_ref, lr_ref, o_ref, s_ref, d):
        fwd = d == 0
        g = _log_sigmoid(jnp.dot(lr_ref[...].astype(BF16), gw_ref[d], preferred_element_type=F32)
                         + gb_ref[d]) / GATE_TAU
        keep = same_chunk & ((cols <= rows) if fwd else (cols >= rows))
        tri = keep.astype(BF16)
        g_hi, g_lo = _split_bf16(g)
        bcum = jnp.dot(tri, g_hi, preferred_element_type=F32) + jnp.dot(tri, g_lo, preferred_element_type=F32)
        ends = [bcum[(c * ck + ck - 1 if fwd else c * ck):(c * ck + ck if fwd else c * ck + 1), :] for c in range(gsz)]
        pad = [jnp.zeros_like(ends[0])] * (SUBLANES - gsz)
        decay = jnp.exp(jnp.concatenate(ends + pad, axis=0))
        decay_t = decay.T
        decay_rows = jnp.concatenate([jnp.broadcast_to(decay[c:c + 1], (ck, decay.shape[1])) for c in range(gsz)], axis=0)
        q = q_ref[...].astype(F32) * scale
        k_e = k_ref[...].astype(F32) * jnp.exp(-bcum)
        q_t = (q * jnp.exp(bcum)).astype(BF16)
        k_t = k_e.astype(BF16)
        k_d = (k_e * decay_rows).astype(BF16)
        order = range(gsz) if fwd else range(gsz - 1, -1, -1)
        for h in range(GLA_HEADS):
            kc = slice(h * dk, (h + 1) * dk)
            vc = slice(h * dv, (h + 1) * dv)
            v = v_ref[:, vc]
            att = lax.dot_general(q_t[:, kc], k_t[:, kc], (((1,), (1,)), ((), ())), preferred_element_type=F32)
            o_intra = jnp.dot(jnp.where(keep, att, 0.0).astype(BF16), v, preferred_element_type=F32)
            s = s_ref[h]
            for c in order:
                rs = slice(c * ck, (c + 1) * ck)
                o = o_intra[rs] + jnp.dot(q_t[rs, kc], s.astype(BF16), preferred_element_type=F32)
                o_ref[rs, vc] = o.astype(o_ref.dtype)
                upd = lax.dot_general(k_d[rs, kc], v[rs], (((0,), (0,)), ((), ())), preferred_element_type=F32)
                s = decay_t[kc, c:c + 1] * s + upd
            s_ref[h] = s

    for bi in range(qf_ref.shape[0]):
        at = lambda *refs: [r.at[bi] for r in refs]
        direction(*at(qf_ref, kf_ref, vf_ref, lrf_ref, of_ref, sf_ref), 0)
        direction(*at(qb_ref, kb_ref, vb_ref, lrb_ref, ob_ref, sb_ref), 1)


def _gla_scan(q, k, v, lr, gate_w, gate_b, s0_f, s0_b, seq_len):
    r = q.shape[0]
    nb = r // seq_len
    dkt = gate_w.shape[2]
    dk = dkt // GLA_HEADS
    dv = s0_f.shape[-1]
    dvt = dv * GLA_HEADS
    gsz = _tile(seq_len // GLA_CHUNK, 4)
    assert gsz <= SUBLANES
    tb = gsz * GLA_CHUNK
    n = seq_len // tb
    nseq = _tile(nb, 4)
    fwd_blk = lambda i: i
    bwd_blk = lambda i: n - 1 - i

    def specs(blk):
        return [pl.BlockSpec((nseq, tb, w), lambda b, i: (b, blk(i), 0)) for w in (dkt, dkt, dvt, LANES)]

    state_spec = pl.BlockSpec((nseq, GLA_HEADS, dk, dv), lambda b, i: (b, 0, 0, 0))
    seqs = [a.reshape(nb, seq_len, a.shape[1]) for a in (q, k, v, lr)]
    o_f, o_b, s_f, s_b = pl.pallas_call(
        functools.partial(_gla_body, dk=dk, dv=dv, gsz=gsz),
        grid=(nb // nseq, n),
        in_specs=specs(fwd_blk) + specs(bwd_blk) + [
            pl.BlockSpec((2, LANES, dkt), lambda b, i: (0, 0, 0)),
            pl.BlockSpec((2, 1, dkt), lambda b, i: (0, 0, 0)),
            state_spec, state_spec,
        ],
        out_specs=[
            pl.BlockSpec((nseq, tb, dvt), lambda b, i: (b, fwd_blk(i), 0)),
            pl.BlockSpec((nseq, tb, dvt), lambda b, i: (b, bwd_blk(i), 0)),
            state_spec, state_spec,
        ],
        out_shape=[jax.ShapeDtypeStruct((nb, seq_len, dvt), BF16), jax.ShapeDtypeStruct((nb, seq_len, dvt), BF16),
                   jax.ShapeDtypeStruct(s0_f.shape, F32), jax.ShapeDtypeStruct(s0_b.shape, F32)],
        compiler_params=_cparams("parallel", "arbitrary"),
        name="gla_scan",
    )(*seqs, *seqs, gate_w, gate_b.reshape(2, 1, dkt), s0_f, s0_b)
    return o_f.reshape(r, dvt), o_b.reshape(r, dvt), s_f, s_b


def _mix_out_body(x_ref, mod_ref, x0_ref, z_ref, r_ref, of_ref, ob_ref, rg_ref, gh_ref, gg_ref,
                  skip_ref, gn_ref, wh_ref, wg_ref, wo_ref, o_ref, *, d, dv):
    f32 = lambda ref: ref[...].astype(F32)
    y_hy = f32(x0_ref) * (f32(r_ref) + skip_ref[...] * f32(z_ref))
    o = f32(of_ref) + f32(ob_ref)
    rg = f32(rg_ref)
    parts = []
    for h in range(GLA_HEADS):
        oh = o[:, h * dv:(h + 1) * dv]
        on = oh * lax.rsqrt(jnp.mean(oh * oh, axis=-1, keepdims=True) + EPS) * gn_ref[...]
        parts.append((on * _silu(rg[:, h * dv:(h + 1) * dv])).astype(BF16))
    y_gla = jnp.concatenate(parts, axis=-1)
    ph = jnp.dot(y_hy.astype(BF16), wh_ref[...], preferred_element_type=F32)
    pg = jnp.dot(y_gla, wg_ref[...], preferred_element_type=F32)
    merged = _sigmoid(f32(gh_ref)) * ph + _sigmoid(f32(gg_ref)) * pg
    out = jnp.dot(merged.astype(BF16), wo_ref[...], preferred_element_type=F32)
    g1 = mod_ref[:, 2 * d:3 * d]
    o_ref[...] = x_ref[...] + g1 * out


def _mix_out(x2, mod_l, x0, z, rconv, o_f, o_b, rg, gh, gg, skip, gn, layer, w_bhy, w_bgla, w_o,
             rows_per_cond, cond_base):
    r, d = x2.shape
    c = x0.shape[1]
    dvt = o_f.shape[1]
    dv = dvt // GLA_HEADS
    tm = _tile(min(r, rows_per_cond), 512)
    bpc = rows_per_cond // tm
    row = lambda i: (i, 0)
    full = lambda shape: pl.BlockSpec(shape, lambda i: (0, 0))
    resident = lambda a: pl.BlockSpec((None,) + a.shape[1:], lambda i: (layer, 0, 0), pipeline_mode=pl.Buffered(1))
    return pl.pallas_call(
        functools.partial(_mix_out_body, d=d, dv=dv),
        grid=(r // tm,),
        in_specs=[
            pl.BlockSpec((tm, d), row),
            pl.BlockSpec((None, 1, N_MOD * d), lambda i: (cond_base + i // bpc, 0, 0)),
            pl.BlockSpec((tm, c), row),
            pl.BlockSpec((tm, c), row),
            pl.BlockSpec((tm, c), row),
            pl.BlockSpec((tm, dvt), row),
            pl.BlockSpec((tm, dvt), row),
            pl.BlockSpec((tm, dvt), row),
            pl.BlockSpec((tm, d), row),
            pl.BlockSpec((tm, d), row),
            full((1, c)), full((1, dv)), resident(w_bhy), resident(w_bgla), resident(w_o),
        ],
        out_specs=pl.BlockSpec((tm, d), row),
        out_shape=jax.ShapeDtypeStruct((r, d), F32),
        compiler_params=_cparams("parallel"),
        name="mix_out",
    )(x2, mod_l, x0, z, rconv, o_f, o_b, rg, gh, gg, skip, gn, w_bhy, w_bgla, w_o)


FF_CHUNK = 1024


def _mlp_body(x_ref, mod_ref, g_ref, w1_ref, w2_ref, fg_ref, o_ref, h_scr, *, d, final_norm):
    h_scr[...] = _modulated_norm(x_ref[...], g_ref[...], mod_ref[...], 3, 4, d).astype(BF16)
    dff = w1_ref.shape[1]
    acc = None
    for f0 in range(0, dff, FF_CHUNK):
        ff = slice(f0, min(f0 + FF_CHUNK, dff))
        a = jnp.maximum(jnp.dot(h_scr[...], w1_ref[:, ff], preferred_element_type=F32), 0.0)
        t = jnp.dot((a * a).astype(BF16), w2_ref[ff, :], preferred_element_type=F32)
        acc = t if acc is None else acc + t
    g2 = mod_ref[:, 5 * d:6 * d]
    y = x_ref[...] + g2 * acc
    if final_norm:
        y = y * lax.rsqrt(jnp.mean(y * y, axis=-1, keepdims=True) + EPS) * fg_ref[...]
    o_ref[...] = y


def _mlp(x2, mod_l, norm_g, layer, w1, w2, final_g, rows_per_cond, cond_base, final_norm):
    r, d = x2.shape
    tm = _tile(min(r, rows_per_cond), 1024)
    bpc = rows_per_cond // tm
    resident = lambda a: pl.BlockSpec((None,) + a.shape[1:], lambda i: (layer, 0, 0), pipeline_mode=pl.Buffered(1))
    return pl.pallas_call(
        functools.partial(_mlp_body, d=d, final_norm=final_norm),
        grid=(r // tm,),
        in_specs=[
            pl.BlockSpec((tm, d), lambda i: (i, 0)),
            pl.BlockSpec((None, 1, N_MOD * d), lambda i: (cond_base + i // bpc, 0, 0)),
            pl.BlockSpec((1, d), lambda i: (0, 0)),
            resident(w1), resident(w2),
            pl.BlockSpec((1, d), lambda i: (0, 0)),
        ],
        out_specs=pl.BlockSpec((tm, d), lambda i: (i, 0)),
        out_shape=jax.ShapeDtypeStruct((r, d), F32),
        scratch_shapes=[pltpu.VMEM((tm, d), BF16)],
        compiler_params=_cparams("parallel"),
        name="mlp",
    )(x2, mod_l, norm_g, w1, w2, final_g)


def kernel(x, c, ctx, c_ctx, ada_w, ada_b, norm1_g, norm2_g, w_in, hy_conv_w, hy_conv_b, hy_filt_w1, hy_filt_b1, hy_filt_w2, hy_filt_b2, hy_filt_w3, hy_filt_freq, hy_decay, hy_skip, gla_gate_w, gla_gate_b, gla_norm_g, w_branch_hy, w_branch_gla, w_out, mlp_w1, mlp_w2, final_g):
    nb, seq, d = x.shape
    lctx = ctx.shape[1]
    depth = ada_w.shape[0]
    chy = hy_decay.shape[-1]
    rank = gla_gate_w.shape[2]
    dkt = gla_gate_w.shape[3]
    dvt = w_branch_gla.shape[1]
    dk, dv = dkt // GLA_HEADS, dvt // GLA_HEADS
    assert nb + 1 <= COND_ROWS and 2 * rank <= LANES

    cond = jnp.zeros((COND_ROWS, d), F32).at[:nb].set(c).at[nb].set(c_ctx)
    mod = _ada_mod(cond, ada_w, ada_b).reshape(depth, COND_ROWS, 1, N_MOD * d)

    sizes = (3 * chy, dkt, dkt, dvt, dvt, rank, rank, d, d)
    offs = np.concatenate([[0], np.cumsum(sizes)])
    w_hy = w_in[..., :offs[1]].astype(BF16)
    w_main = jnp.concatenate([w_in[..., offs[1]:offs[5]], w_in[..., offs[7]:]], axis=-1).astype(BF16)
    w_lr = jnp.pad(w_in[..., offs[5]:offs[7]], ((0, 0), (0, 0), (0, LANES - 2 * rank))).astype(BF16)
    conv_b = hy_conv_b.reshape(depth, 1, 3 * chy)
    seg_widths = (dkt, dkt, dvt, dvt, d, d)

    gate_w_pad = jnp.zeros((depth, 2, LANES, dkt), F32)
    gate_w_pad = gate_w_pad.at[:, 0, :rank].set(gla_gate_w[:, 0]).at[:, 1, rank:2 * rank].set(gla_gate_w[:, 1])
    gate_w_pad = gate_w_pad.astype(BF16)

    w_bhy = w_branch_hy.astype(BF16)
    w_bgla = w_branch_gla.astype(BF16)
    w_o = w_out.astype(BF16)
    w1 = mlp_w1.astype(BF16)
    w2 = mlp_w2.astype(BF16)

    dft_lat, dft_ctx = _dft_constants(seq), _dft_constants(lctx)
    feat_lat, feat_ctx = _filter_features(seq), _filter_features(lctx)
    zero_state = jnp.zeros((nb, GLA_HEADS, dk, dv), F32)

    xs = x.reshape(nb * seq, d)
    cs = ctx.reshape(nb * lctx, d)
    for l in range(depth):
        last = l == depth - 1
        mod_l = mod[l]
        n1 = norm1_g[l].reshape(1, d)
        filt = (hy_filt_w1[l], hy_filt_b1[l], hy_filt_w2[l], hy_filt_b2[l], hy_filt_w3[l],
                hy_filt_freq[l], hy_decay[l])
        skip = hy_skip[l].reshape(1, chy)
        gn = gla_norm_g[l].reshape(1, dv)

        in_w = (l, w_hy, hy_conv_w, conv_b, w_main, w_lr, seg_widths)
        out_w = (l, w_bhy, w_bgla, w_o)
        gate = (gate_w_pad[l], gla_gate_b[l])
        x0_c, z_c, lr_c, q_c, k_c, v_c, *gates_c = _in_proj(cs, mod_l, n1, *in_w, nb * lctx, nb, lctx)
        of_c, ob_c, sf_c, sb_c = _gla_scan(q_c, k_c, v_c, lr_c, *gate, zero_state, zero_state, lctx)
        x0_l, z_l, lr_l, q_l, k_l, v_l, *gates_l = _in_proj(xs, mod_l, n1, *in_w, seq, 0, GRID_W)
        of_l, ob_l, _, _ = _gla_scan(q_l, k_l, v_l, lr_l, *gate, sf_c, sb_c, seq)

        h_l = _filter_spectrum(_hyena_filter_parts(feat_lat, *filt), dft_lat, seq)
        r_l = _long_conv(z_l, h_l, dft_lat, seq)
        xs = _mix_out(xs, mod_l, x0_l, z_l, r_l, of_l, ob_l, *gates_l, skip, gn, *out_w, seq, 0)
        if not last:
            h_c = _filter_spectrum(_hyena_filter_parts(feat_ctx, *filt), dft_ctx, lctx)
            r_c = _long_conv(z_c, h_c, dft_ctx, lctx)
            cs = _mix_out(cs, mod_l, x0_c, z_c, r_c, of_c, ob_c, *gates_c, skip, gn, *out_w, nb * lctx, nb)

        n2 = norm2_g[l].reshape(1, d)
        fg = final_g.reshape(1, d)
        xs = _mlp(xs, mod_l, n2, l, w1, w2, fg, seq, 0, last)
        if not last:
            cs = _mlp(cs, mod_l, n2, l, w1, w2, fg, nb * lctx, nb, False)
    return xs.reshape(nb, seq, d)
```
